```python
import jax, jax.numpy as jnp
from jax import lax
import numpy as np

D_MODEL = 1024
BATCH = 8
SEQ = 2048
DEPTH = 4
DEC_BATCH = 128
DEC_SEQ = 8
PAST_LEN = 16384
PAGE_SIZE = 128

D_MIX = D_MODEL
D_LRU = D_MIX // 2
N_LRU_BLOCKS = 8
LRU_BLOCK = D_LRU // N_LRU_BLOCKS
CONV_W = 4
LRU_C = 8.0
D_GLA = D_MIX - D_LRU
GLA_HEADS = 4
GLA_DV = D_GLA // GLA_HEADS
GLA_DK = GLA_DV // 2
GLA_RANK = 16
GLA_NORMALIZER = 16.0
GLA_CHUNK = 64
D_FF = 2816
EPS = 1e-6
IN_WIDTHS = (D_LRU, D_LRU, GLA_HEADS * GLA_DK, GLA_HEADS * GLA_DK, D_GLA, D_GLA, GLA_RANK)
IN_SPLITS = tuple(int(s) for s in np.cumsum(IN_WIDTHS)[:-1])
D_IN = int(sum(IN_WIDTHS))

kernel_name = "hymba_rglru_gla_macaron_decode_step"


def rmsnorm(x, g):
    xf = x.astype(jnp.float32)
    y = xf * lax.rsqrt(jnp.mean(xf * xf, axis=-1, keepdims=True) + EPS)
    return (y * g.astype(jnp.float32)).astype(x.dtype)


def swiglu(h, w_gate, w_up, w_down):
    return (jax.nn.silu(h @ w_gate) * (h @ w_up)) @ w_down


def causal_conv(x, buf, w, b):
    T = x.shape[1]
    xp = jnp.concatenate([buf.astype(x.dtype), x], axis=1)
    out = sum(xp[:, j:j + T] * w[j] for j in range(CONV_W)) + b
    return out, xp[:, xp.shape[1] - (CONV_W - 1):]


def rg_lru(xc, h0, reset_first, wa, ba, wx, bx, lam):
    B, T, _ = xc.shape
    xf = xc.astype(jnp.float32)
    xb = xf.reshape(B, T, N_LRU_BLOCKS, LRU_BLOCK)
    r = jax.nn.sigmoid(jnp.einsum('btnd,nde->btne', xb, wa.astype(jnp.float32)).reshape(B, T, D_LRU) + ba)
    i = jax.nn.sigmoid(jnp.einsum('btnd,nde->btne', xb, wx.astype(jnp.float32)).reshape(B, T, D_LRU) + bx)
    log_a = -LRU_C * r * jax.nn.softplus(-lam.astype(jnp.float32))
    a = jnp.exp(log_a)
    mult = jnp.sqrt(-jnp.expm1(2.0 * log_a))
    if reset_first:
        mult = mult.at[:, 0].set(1.0)
    bterm = mult * i * xf
    bterm = bterm.at[:, 0].add(a[:, 0] * h0.astype(jnp.float32))

    def combine(left, right):
        a_l, b_l = left
        a_r, b_r = right
        return a_l * a_r, a_r * b_l + b_r

    _, h = lax.associative_scan(combine, (a, bterm), axis=1)
    return h, h[:, -1]


def gla_chunked(q, k, v, gk, S0):
    B, T = q.shape[0], q.shape[1]
    C = GLA_CHUNK if T % GLA_CHUNK == 0 else T
    n = T // C

    def to_chunks(z):
        return z.reshape(B, n, C, GLA_HEADS, z.shape[-1]).transpose(1, 0, 3, 2, 4)

    mask = jnp.tril(jnp.ones((C, C), dtype=bool))[:, :, None]

    def step(S, inp):
        qc, kc, vc, gc = inp
        bcum = jnp.cumsum(gc, axis=2)
        diff = bcum[:, :, :, None, :] - bcum[:, :, None, :, :]
        decay = jnp.exp(jnp.where(mask, diff, -jnp.inf))
        scores = jnp.einsum('bhtd,bhsd,bhtsd->bhts', qc, kc, decay)
        o = jnp.einsum('bhts,bhsv->bhtv', scores, vc) + jnp.einsum('bhtd,bhdv->bhtv', qc * jnp.exp(bcum), S)
        blast = bcum[:, :, -1:]
        S_new = jnp.exp(blast[:, :, 0])[..., None] * S + jnp.einsum('bhsd,bhsv->bhdv', kc * jnp.exp(blast - bcum), vc)
        return S_new, o

    S_fin, o = lax.scan(step, S0.astype(jnp.float32), (to_chunks(q), to_chunks(k), to_chunks(v), to_chunks(gk)))
    o = o.transpose(1, 0, 3, 2, 4).reshape(B, T, GLA_HEADS, GLA_DV)
    return o, S_fin


def mixer(h, st_h, st_conv, st_gla, reset_first, p):
    B, T, _ = h.shape
    proj = h @ p['w_in']
    xl, yl, q, k, v, g, gd = jnp.split(proj, IN_SPLITS, axis=-1)
    xc, new_conv = causal_conv(xl, st_conv, p['conv_w'], p['conv_b'])
    hl, h_last = rg_lru(xc, st_h, reset_first, p['lru_wa'], p['lru_ba'], p['lru_wx'], p['lru_bx'], p['lru_lambda'])
    lru_out = rmsnorm((hl * jax.nn.gelu(yl.astype(jnp.float32))).astype(h.dtype), p['lru_norm'])
    gk = jax.nn.log_sigmoid((gd @ p['gla_w_gate'] + p['gla_b_gate']).astype(jnp.float32)) / GLA_NORMALIZER
    qh = q.astype(jnp.float32).reshape(B, T, GLA_HEADS, GLA_DK) * (GLA_DK ** -0.5)
    kh = k.astype(jnp.float32).reshape(B, T, GLA_HEADS, GLA_DK)
    vh = v.astype(jnp.float32).reshape(B, T, GLA_HEADS, GLA_DV)
    o, S_fin = gla_chunked(qh, kh, vh, gk.reshape(B, T, GLA_HEADS, GLA_DK), st_gla)
    o = rmsnorm(o, p['gla_norm']).reshape(B, T, D_GLA)
    gla_out = (o * jax.nn.silu(g.astype(jnp.float32))).astype(h.dtype)
    out = jnp.concatenate([lru_out, gla_out], axis=-1) @ p['w_out']
    return out, h_last, new_conv, S_fin


def layer(x, st_h, st_conv, st_gla, reset_first, p):
    h = rmsnorm(x, p['norm_pre'][0])
    x = x + 0.5 * rmsnorm(swiglu(h, p['w_ffn_gate'][0], p['w_ffn_up'][0], p['w_ffn_down'][0]), p['norm_post'][0])
    h = rmsnorm(x, p['norm_pre'][1])
    m, h_last, new_conv, S_fin = mixer(h, st_h, st_conv, st_gla, reset_first, p)
    x = x + rmsnorm(m, p['norm_post'][1])
    h = rmsnorm(x, p['norm_pre'][2])
    x = x + 0.5 * rmsnorm(swiglu(h, p['w_ffn_gate'][1], p['w_ffn_up'][1], p['w_ffn_down'][1]), p['norm_post'][2])
    return x, h_last, new_conv, S_fin


def setup_inputs(seed: int = 0) -> dict:
    key = jax.random.key(seed)
    ks = jax.random.split(key, 24)
    f32 = jnp.float32
    nrm = lambda k, shape, s: (jax.random.normal(k, shape, f32) * s)
    a0 = jax.random.uniform(ks[15], (DEPTH, D_LRU), f32, 0.9, 0.999)
    return {
        'x_prompt': nrm(ks[0], (BATCH, SEQ, D_MODEL), 1.0),
        'x_sample': nrm(ks[1], (DEC_BATCH, DEC_SEQ, D_MODEL), 1.0),
        'state_lru_h': nrm(ks[2], (DEPTH, DEC_BATCH, D_LRU), 0.5),
        'state_lru_conv': nrm(ks[3], (DEPTH, DEC_BATCH, CONV_W - 1, D_LRU), 1.0),
        'state_gla': nrm(ks[4], (DEPTH, DEC_BATCH, GLA_HEADS, GLA_DK, GLA_DV), 0.5),
        'norm_pre': 1.0 + nrm(ks[5], (DEPTH, 3, D_MODEL), 0.05),
        'norm_post': 1.0 + nrm(ks[6], (DEPTH, 3, D_MODEL), 0.05),
        'w_ffn_gate': nrm(ks[7], (DEPTH, 2, D_MODEL, D_FF), D_MODEL ** -0.5),
        'w_ffn_up': nrm(ks[8], (DEPTH, 2, D_MODEL, D_FF), D_MODEL ** -0.5),
        'w_ffn_down': nrm(ks[9], (DEPTH, 2, D_FF, D_MODEL), D_FF ** -0.5),
        'w_in': nrm(ks[10], (DEPTH, D_MODEL, D_IN), D_MODEL ** -0.5),
        'conv_w': nrm(ks[11], (DEPTH, CONV_W, D_LRU), CONV_W ** -0.5),
        'conv_b': nrm(ks[12], (DEPTH, D_LRU), 0.01),
        'lru_wa': nrm(ks[13], (DEPTH, N_LRU_BLOCKS, LRU_BLOCK, LRU_BLOCK), LRU_BLOCK ** -0.5),
        'lru_ba': nrm(ks[14], (DEPTH, D_LRU), 0.01),
        'lru_wx': nrm(ks[16], (DEPTH, N_LRU_BLOCKS, LRU_BLOCK, LRU_BLOCK), LRU_BLOCK ** -0.5),
        'lru_bx': nrm(ks[17], (DEPTH, D_LRU), 0.01),
        'lru_lambda': jnp.log(a0) - jnp.log1p(-a0),
        'lru_norm': 1.0 + nrm(ks[18], (DEPTH, D_LRU), 0.05),
        'gla_w_gate': nrm(ks[19], (DEPTH, GLA_RANK, GLA_HEADS * GLA_DK), GLA_RANK ** -0.5),
        'gla_b_gate': nrm(ks[20], (DEPTH, GLA_HEADS * GLA_DK), 0.01),
        'gla_norm': 1.0 + nrm(ks[21], (DEPTH, GLA_DV), 0.05),
        'w_out': nrm(ks[22], (DEPTH, D_MIX, D_MODEL), D_MIX ** -0.5),
    }


def reference(x_prompt, x_sample, state_lru_h, state_lru_conv, state_gla,
              norm_pre, norm_post, w_ffn_gate, w_ffn_up, w_ffn_down, w_in,
              conv_w, conv_b, lru_wa, lru_ba, lru_wx, lru_bx, lru_lambda, lru_norm,
              gla_w_gate, gla_b_gate, gla_norm, w_out):
    B = x_prompt.shape[0]
    yp, ys = x_prompt, x_sample
    hp_list, cp_list, sp_list = [], [], []
    hs_list, cs_list, ss_list = [], [], []
    for l in range(DEPTH):
        p = {
            'norm_pre': norm_pre[l], 'norm_post': norm_post[l],
            'w_ffn_gate': w_ffn_gate[l], 'w_ffn_up': w_ffn_up[l], 'w_ffn_down': w_ffn_down[l],
            'w_in': w_in[l], 'conv_w': conv_w[l], 'conv_b': conv_b[l],
            'lru_wa': lru_wa[l], 'lru_ba': lru_ba[l], 'lru_wx': lru_wx[l], 'lru_bx': lru_bx[l],
            'lru_lambda': lru_lambda[l], 'lru_norm': lru_norm[l],
            'gla_w_gate': gla_w_gate[l], 'gla_b_gate': gla_b_gate[l], 'gla_norm': gla_norm[l],
            'w_out': w_out[l],
        }
        zero_h = jnp.zeros((B, D_LRU), jnp.float32)
        zero_conv = jnp.zeros((B, CONV_W - 1, D_LRU), x_prompt.dtype)
        zero_gla = jnp.zeros((B, GLA_HEADS, GLA_DK, GLA_DV), jnp.float32)
        yp, hp, cp, sp = layer(yp, zero_h, zero_conv, zero_gla, True, p)
        ys, hs, cs, ss = layer(ys, state_lru_h[l], state_lru_conv[l], state_gla[l], False, p)
        hp_list.append(hp); cp_list.append(cp); sp_list.append(sp)
        hs_list.append(hs); cs_list.append(cs); ss_list.append(ss)
    new_lru_h_prompt = jnp.stack(hp_list, axis=0)
    new_lru_conv_prompt = jnp.stack(cp_list, axis=0)
    new_gla_prompt = jnp.stack(sp_list, axis=0)
    new_lru_h_sample = jnp.stack(hs_list, axis=0)
    new_lru_conv_sample = jnp.stack(cs_list, axis=0)
    new_gla_sample = jnp.stack(ss_list, axis=0)
    return (yp, ys, new_lru_h_prompt, new_lru_conv_prompt, new_gla_prompt,
            new_lru_h_sample, new_lru_conv_sample, new_gla_sample)
```

```python
import functools

import jax
import jax.numpy as jnp
from jax import lax
from jax.experimental import pallas as pl
from jax.experimental.pallas import tpu as pltpu

D_MODEL = 1024
BATCH = 8
SEQ = 2048
DEPTH = 4
DEC_BATCH = 128
DEC_SEQ = 8
D_LRU = 512
N_LRU_BLOCKS = 8
LRU_BLOCK = 64
CONV_W = 4
LRU_C = 8.0
D_GLA = 512
GLA_HEADS = 4
GLA_DV = 128
GLA_DK = 64
GLA_RANK = 16
GLA_NORMALIZER = 16.0
GLA_CHUNK = 64
D_FF = 2816
EPS = 1e-6
D_QK = GLA_HEADS * GLA_DK

N_PROMPT = BATCH * SEQ
N_SAMPLE = DEC_BATCH * DEC_SEQ
N_ROWS = N_PROMPT + N_SAMPLE

SUBLANES = 8
LANES = 128
MXU_DIM = 256
VMEM_LIMIT = 56 * 1024 * 1024

FFN_TM = 512
FFN_FC = 256
PROMPT_TT = 512
SAMPLE_BT = 32
BLK = 64

F32 = jnp.float32
BF16 = jnp.bfloat16


def _rms(x, g):
    ms = jnp.mean(x * x, axis=-1, keepdims=True)
    return x * lax.rsqrt(ms + EPS) * g


def _softplus(x):
    return jnp.maximum(x, 0.0) + jnp.log1p(jnp.exp(-jnp.abs(x)))


def _dot(a, b):
    return jnp.dot(a, b, preferred_element_type=F32)


def _dot_nt(a, b):
    return lax.dot_general(a, b, (((1,), (1,)), ((), ())), preferred_element_type=F32)


def _dot_tn(a, b):
    return lax.dot_general(a, b, (((0,), (0,)), ((), ())), preferred_element_type=F32)


def _row_iota(shape):
    return lax.broadcasted_iota(jnp.int32, shape, 0)


def _ffn_kernel(x_ref, gpre_ref, gpost_ref, wg_ref, wu_ref, wd_ref, o_ref):
    x = x_ref[...]
    h = _rms(x, gpre_ref[...]).astype(BF16)
    acc = jnp.zeros((x.shape[0], D_MODEL), F32)
    for c0 in range(0, D_FF, FFN_FC):
        g = _dot(h, wg_ref[:, c0:c0 + FFN_FC])
        u = _dot(h, wu_ref[:, c0:c0 + FFN_FC])
        act = (jax.nn.silu(g) * u).astype(BF16)
        acc = acc + _dot(act, wd_ref[c0:c0 + FFN_FC, :])
    o_ref[...] = x + 0.5 * _rms(acc, gpost_ref[...])


def _ffn_call(x_all, gpre4, gpost4, wg, wu, wd, l, idx, sub):
    const = lambda *ix: (lambda i: ix)
    nidx = 2 * idx
    return pl.pallas_call(
        _ffn_kernel,
        grid=(N_ROWS // FFN_TM,),
        in_specs=[
            pl.BlockSpec((FFN_TM, D_MODEL), lambda i: (i, 0)),
            pl.BlockSpec((None, None, 1, D_MODEL), const(l, nidx, 0, 0)),
            pl.BlockSpec((None, None, 1, D_MODEL), const(l, nidx, 0, 0)),
            pl.BlockSpec((None, None, D_MODEL, D_FF), const(l, idx, 0, 0),
                         pipeline_mode=pl.Buffered(1)),
            pl.BlockSpec((None, None, D_MODEL, D_FF), const(l, idx, 0, 0),
                         pipeline_mode=pl.Buffered(1)),
            pl.BlockSpec((None, None, D_FF, D_MODEL), const(l, idx, 0, 0),
                         pipeline_mode=pl.Buffered(1)),
        ],
        out_specs=pl.BlockSpec((FFN_TM, D_MODEL), lambda i: (i, 0)),
        out_shape=jax.ShapeDtypeStruct((N_ROWS, D_MODEL), F32),
        input_output_aliases={0: 0},
        compiler_params=pltpu.CompilerParams(
            dimension_semantics=("parallel",), vmem_limit_bytes=VMEM_LIMIT),
        name=f"ffn_{sub}",
    )(x_all, gpre4, gpost4, wg, wu, wd)


def _seg_scan_affine(a, b, seg):
    t = _row_iota(a.shape) & (seg - 1)
    d = 1
    while d < seg:
        m = t >= d
        b = jnp.where(m, a * pltpu.roll(b, d, 0) + b, b)
        a = jnp.where(m, a * pltpu.roll(a, d, 0), a)
        d *= 2
    return a, b


def _seg_cumsum(x, seg):
    t = _row_iota(x.shape) & (seg - 1)
    d = 1
    while d < seg:
        x = jnp.where(t >= d, x + pltpu.roll(x, d, 0), x)
        d *= 2
    return x


def _conv_shift(x, tail, k, seg):
    n = x.shape[0]
    xs = pltpu.roll(x, k, 0)
    nt = tail.shape[0]
    ts = pltpu.roll(tail, (nt + k - SUBLANES) % nt, 0)
    if seg == SUBLANES:
        t = _row_iota(x.shape) & (SUBLANES - 1)
        return jnp.where(t < k, ts, xs)
    t = _row_iota((SUBLANES, x.shape[1]))
    first = jnp.where(t < k, ts, xs[:SUBLANES])
    return jnp.concatenate([first, xs[SUBLANES:]], axis=0)


def _lru_gates(xc, vec_ref, wa_ref, wx_ref):
    half = D_LRU // 2
    xb = xc.astype(BF16)
    xb0, xb1 = xb[:, :half], xb[:, half:]
    ba, bx, lam = vec_ref[1:2, :], vec_ref[2:3, :], vec_ref[3:4, :]
    r = jax.nn.sigmoid(
        jnp.concatenate([_dot(xb0, wa_ref[0]), _dot(xb1, wa_ref[1])], axis=1) + ba)
    i = jax.nn.sigmoid(
        jnp.concatenate([_dot(xb0, wx_ref[0]), _dot(xb1, wx_ref[1])], axis=1) + bx)
    log_a = -LRU_C * r * _softplus(-lam)
    a = jnp.exp(log_a)
    mult = jnp.sqrt(-jnp.tanh(log_a) * (a * a + 1.0))
    return a, mult, i


def _level_masks(nlev):
    t = lax.broadcasted_iota(jnp.int32, (BLK, BLK), 0)
    s = lax.broadcasted_iota(jnp.int32, (BLK, BLK), 1)
    masks = [t == s]
    for lv in range(nlev):
        masks.append((((t >> lv) ^ (s >> lv)) == 1) & (((t >> lv) & 1) == 1))
    return masks


def _gla_prepare(q, k, gk, seg, qm_ref, kt_ref, qe_ref, ke_ref, bl_ref):
    n = q.shape[0]
    t = _row_iota(gk.shape)
    lane_half = (lax.broadcasted_iota(jnp.int32, gk.shape, 1) & (LANES - 1)) >= GLA_DK
    b = _seg_cumsum(gk, seg)

    def store_q(dst, lv, val):
        zero = jnp.zeros_like(val)
        if lv is None:
            dst[0] = jnp.where(lane_half, zero, val).astype(dst.dtype)
            dst[1] = jnp.where(lane_half, val, zero).astype(dst.dtype)
        else:
            dst[lv, 0] = jnp.where(lane_half, zero, val).astype(dst.dtype)
            dst[lv, 1] = jnp.where(lane_half, val, zero).astype(dst.dtype)

    store_q(qm_ref, 0, q)
    kt_ref[0] = k.astype(kt_ref.dtype)
    ref = b - gk
    end = b
    s = 1
    lv = 1
    while s < seg:
        store_q(qm_ref, lv, q * jnp.exp(b - ref))
        kt_ref[lv] = (k * jnp.exp(end - b)).astype(kt_ref.dtype)
        bit = (t & s) != 0
        ref = jnp.where(bit, pltpu.roll(ref, s, 0), ref)
        end = jnp.where(bit, end, pltpu.roll(end, n - s, 0))
        s *= 2
        lv += 1
    store_q(qe_ref, None, q * jnp.exp(b))
    store_q(ke_ref, None, k * jnp.exp(end - b))
    bl_ref[...] = end


def _block_scores(qm_ref, kt_ref, masks, r0, h):
    grp, half = divmod(h, 2)
    lanes = slice(LANES * grp, LANES * (grp + 1))
    p = jnp.zeros((BLK, BLK), F32)
    for lv, m in enumerate(masks):
        sc = _dot_nt(qm_ref[lv, half, pl.ds(r0, BLK), lanes],
                     kt_ref[lv, pl.ds(r0, BLK), lanes])
        p = jnp.where(m, sc, p)
    return p


def _decay_rows(bl_row):
    return jnp.exp(jnp.transpose(jnp.broadcast_to(bl_row, (LANES, LANES))))


def _gla_finish(o, g, gn):
    outs = []
    for h in range(GLA_HEADS):
        sl = slice(GLA_DV * h, GLA_DV * (h + 1))
        outs.append(_rms(o[:, sl], gn) * jax.nn.silu(g[:, sl]))
    return jnp.concatenate(outs, axis=1)


def _project(x, gpre_ref, w_lru_ref, w_qk_ref, w_v_ref, w_g_ref, w_gd_ref):
    hn = _rms(x, gpre_ref[...]).astype(BF16)
    xy = _dot(hn, w_lru_ref[...])
    qk = _dot(hn, w_qk_ref[...])
    v = _dot(hn, w_v_ref[...])
    g = _dot(hn, w_g_ref[...])
    gd = _dot(hn, w_gd_ref[...])
    return xy[:, :D_LRU], xy[:, D_LRU:], qk[:, :D_QK] * (GLA_DK ** -0.5), qk[:, D_QK:], v, g, gd


def _gk_from(gd, wgate_ref, bgate_ref):
    z = _dot(gd.astype(BF16), wgate_ref[...]) + bgate_ref[...]
    return -_softplus(-z) * (1.0 / GLA_NORMALIZER)


def _out_proj(x, lru_out, gla_out, w_out_ref, gpost_ref):
    m = (_dot(lru_out.astype(BF16), w_out_ref[:D_LRU, :])
         + _dot(gla_out.astype(BF16), w_out_ref[D_LRU:, :]))
    return x + _rms(m, gpost_ref[...])


def _prompt_mixer_kernel(
        x_ref, gpre_ref, gpost_ref, w_lru_ref, w_qk_ref, w_v_ref, w_g_ref, w_gd_ref,
        convw_ref, vec_ref, wa_ref, wx_ref, wgate_ref, bgate_ref, gn_ref, w_out_ref,
        o_ref, hlast_ref, ctail_ref, sfin_ref,
        tail_sc, h_sc, s_sc, qm_sc, kt_sc, qe_sc, ke_sc, bl_sc, v_sc, o_sc):
    j = pl.program_id(1)
    tt = PROMPT_TT

    @pl.when(j == 0)
    def _():
        tail_sc[...] = jnp.zeros_like(tail_sc)
        h_sc[...] = jnp.zeros_like(h_sc)
        s_sc[...] = jnp.zeros_like(s_sc)

    x = x_ref[...]
    xl, yl, q, k, v, g, gd = _project(
        x, gpre_ref, w_lru_ref, w_qk_ref, w_v_ref, w_g_ref, w_gd_ref)

    tail = tail_sc[...]
    xc = None
    for jj in range(CONV_W):
        kk = CONV_W - 1 - jj
        xs = xl if kk == 0 else _conv_shift(xl, tail, kk, tt)
        term = xs * convw_ref[jj:jj + 1, :]
        xc = term if xc is None else xc + term
    xc = xc + vec_ref[0:1, :]
    tail_sc[...] = xl[tt - SUBLANES:, :]
    ctail_ref[...] = xl[tt - SUBLANES:, :]

    a, mult, gi = _lru_gates(xc, vec_ref, wa_ref, wx_ref)
    first_row = (_row_iota(a.shape) == 0) & (j == 0)
    mult = jnp.where(first_row, 1.0, mult)
    a8, h8 = _seg_scan_affine(a, mult * gi * xc, SUBLANES)
    hin = h_sc[SUBLANES - 1:SUBLANES, :]
    hs = []
    for gidx in range(tt // SUBLANES):
        rows = slice(SUBLANES * gidx, SUBLANES * (gidx + 1))
        hg = a8[rows] * hin + h8[rows]
        hin = hg[SUBLANES - 1:SUBLANES, :]
        hs.append(hg)
    hseq = jnp.concatenate(hs, axis=0)
    h_sc[...] = hs[-1]
    hlast_ref[...] = hs[-1]
    lru_out = _rms(hseq * jax.nn.gelu(yl), vec_ref[4:5, :])

    gk = _gk_from(gd, wgate_ref, bgate_ref)
    _gla_prepare(q, k, gk, GLA_CHUNK, qm_sc, kt_sc, qe_sc, ke_sc, bl_sc)
    v_sc[...] = v.astype(BF16)
    masks = _level_masks(6)

    def chunk_body(c, carry):
        r0 = pl.multiple_of(c * BLK, BLK)
        for h in range(GLA_HEADS):
            grp, half = divmod(h, 2)
            lanes = slice(LANES * grp, LANES * (grp + 1))
            p = _block_scores(qm_sc, kt_sc, masks, r0, h)
            vh = v_sc[pl.ds(r0, BLK), GLA_DV * h:GLA_DV * (h + 1)]
            s_h = s_sc[h]
            o = (_dot(p.astype(BF16), vh)
                 + _dot(qe_sc[half, pl.ds(r0, BLK), lanes], s_h.astype(BF16)))
            o_sc[pl.ds(r0, BLK), GLA_DV * h:GLA_DV * (h + 1)] = o
            upd = _dot_tn(ke_sc[half, pl.ds(r0, BLK), lanes], vh)
            dec = _decay_rows(bl_sc[pl.ds(r0, 1), lanes])
            s_sc[h] = dec * s_h + upd
        return carry

    lax.fori_loop(0, tt // BLK, chunk_body, 0)
    for h in range(GLA_HEADS):
        half = h % 2
        sfin_ref[h] = s_sc[h, GLA_DK * half:GLA_DK * (half + 1), :]
    gla_out = _gla_finish(o_sc[...], g, gn_ref[...])

    o_ref[...] = _out_proj(x, lru_out, gla_out, w_out_ref, gpost_ref)


def _weight_specs(l, const):
    d = D_MODEL
    return [
        pl.BlockSpec((None, None, 1, d), const(l, 1, 0, 0)),
        pl.BlockSpec((None, None, 1, d), const(l, 1, 0, 0)),
        pl.BlockSpec((None, d, 2 * D_LRU), const(l, 0, 0)),
        pl.BlockSpec((None, d, 2 * D_QK), const(l, 0, 0)),
        pl.BlockSpec((None, d, D_GLA), const(l, 0, 0)),
        pl.BlockSpec((None, d, D_GLA), const(l, 0, 0)),
        pl.BlockSpec((None, d, GLA_RANK), const(l, 0, 0)),
        pl.BlockSpec((None, CONV_W, D_LRU), const(l, 0, 0)),
        pl.BlockSpec((None, 5, D_LRU), const(l, 0, 0)),
        pl.BlockSpec((None, 2, MXU_DIM, MXU_DIM), const(l, 0, 0, 0)),
        pl.BlockSpec((None, 2, MXU_DIM, MXU_DIM), const(l, 0, 0, 0)),
        pl.BlockSpec((None, GLA_RANK, D_QK), const(l, 0, 0)),
        pl.BlockSpec((None, 1, D_QK), const(l, 0, 0)),
        pl.BlockSpec((None, 1, GLA_DV), const(l, 0, 0)),
        pl.BlockSpec((None, d, d), const(l, 0, 0)),
    ]


def _prompt_mixer_call(x_all, weights, l):
    tt = PROMPT_TT
    nt = SEQ // tt
    const = lambda *ix: (lambda b, j: ix)
    out_shapes = (
        jax.ShapeDtypeStruct((N_ROWS, D_MODEL), F32),
        jax.ShapeDtypeStruct((BATCH, SUBLANES, D_LRU), F32),
        jax.ShapeDtypeStruct((BATCH, SUBLANES, D_LRU), F32),
        jax.ShapeDtypeStruct((BATCH, GLA_HEADS, GLA_DK, GLA_DV), F32),
    )
    out_specs = (
        pl.BlockSpec((tt, D_MODEL), lambda b, j: (b * nt + j, 0)),
        pl.BlockSpec((None, SUBLANES, D_LRU), lambda b, j: (b, 0, 0)),
        pl.BlockSpec((None, SUBLANES, D_LRU), lambda b, j: (b, 0, 0)),
        pl.BlockSpec((None, GLA_HEADS, GLA_DK, GLA_DV), lambda b, j: (b, 0, 0, 0)),
    )
    nlev = 7
    scratch = [
        pltpu.VMEM((SUBLANES, D_LRU), F32),
        pltpu.VMEM((SUBLANES, D_LRU), F32),
        pltpu.VMEM((GLA_HEADS, LANES, GLA_DV), F32),
        pltpu.VMEM((nlev, 2, tt, D_QK), BF16),
        pltpu.VMEM((nlev, tt, D_QK), BF16),
        pltpu.VMEM((2, tt, D_QK), BF16),
        pltpu.VMEM((2, tt, D_QK), BF16),
        pltpu.VMEM((tt, D_QK), F32),
        pltpu.VMEM((tt, D_GLA), BF16),
        pltpu.VMEM((tt, D_GLA), F32),
    ]
    return pl.pallas_call(
        _prompt_mixer_kernel,
        grid=(BATCH, nt),
        in_specs=[pl.BlockSpec((tt, D_MODEL), lambda b, j: (b * nt + j, 0))]
                 + _weight_specs(l, const),
        out_specs=out_specs,
        out_shape=out_shapes,
        scratch_shapes=scratch,
        input_output_aliases={0: 0},
        compiler_params=pltpu.CompilerParams(
            dimension_semantics=("parallel", "arbitrary"), vmem_limit_bytes=VMEM_LIMIT),
        name="mixer_prompt",
    )(x_all, *weights)


def _sample_mixer_kernel(
        x_ref, gpre_ref, gpost_ref, w_lru_ref, w_qk_ref, w_v_ref, w_g_ref, w_gd_ref,
        convw_ref, vec_ref, wa_ref, wx_ref, wgate_ref, bgate_ref, gn_ref, w_out_ref,
        h0_ref, tail_ref, s0_ref,
        o_ref, hseq_ref, xl_ref, snew_ref,
        qm_sc, kt_sc, qe_sc, ke_sc, bl_sc, v_sc, vf_sc, o_sc):
    rows = SAMPLE_BT * DEC_SEQ
    x = x_ref[...]
    xl, yl, q, k, v, g, gd = _project(
        x, gpre_ref, w_lru_ref, w_qk_ref, w_v_ref, w_g_ref, w_gd_ref)

    tail = tail_ref[...]
    xc = None
    for jj in range(CONV_W):
        kk = CONV_W - 1 - jj
        xs = xl if kk == 0 else _conv_shift(xl, tail, kk, DEC_SEQ)
        term = xs * convw_ref[jj:jj + 1, :]
        xc = term if xc is None else xc + term
    xc = xc + vec_ref[0:1, :]
    xl_ref[...] = xl

    a, mult, gi = _lru_gates(xc, vec_ref, wa_ref, wx_ref)
    bterm = mult * gi * xc + a * h0_ref[...]
    _, hseq = _seg_scan_affine(a, bterm, DEC_SEQ)
    hseq_ref[...] = hseq
    lru_out = _rms(hseq * jax.nn.gelu(yl), vec_ref[4:5, :])

    gk = _gk_from(gd, wgate_ref, bgate_ref)
    _gla_prepare(q, k, gk, DEC_SEQ, qm_sc, kt_sc, qe_sc, ke_sc, bl_sc)
    v_sc[...] = v.astype(BF16)
    vf_sc[...] = v
    masks = _level_masks(3)

    def block_body(c, carry):
        r0 = pl.multiple_of(c * BLK, BLK)
        for h in range(GLA_HEADS):
            p = _block_scores(qm_sc, kt_sc, masks, r0, h)
            vh = v_sc[pl.ds(r0, BLK), GLA_DV * h:GLA_DV * (h + 1)]
            o_sc[pl.ds(r0, BLK), GLA_DV * h:GLA_DV * (h + 1)] = _dot(p.astype(BF16), vh)
        return carry

    lax.fori_loop(0, rows // BLK, block_body, 0)

    zeros_half = jnp.zeros((GLA_DK, GLA_DV), F32)

    def seq_body(bi, carry):
        r0 = pl.multiple_of(bi * DEC_SEQ, DEC_SEQ)
        for h in range(GLA_HEADS):
            grp, half = divmod(h, 2)
            lanes = slice(LANES * grp, LANES * (grp + 1))
            cols = slice(GLA_DV * h, GLA_DV * (h + 1))
            s0 = s0_ref[bi, h]
            s_pad = jnp.concatenate([zeros_half, s0] if half else [s0, zeros_half], axis=0)
            qe = qe_sc[half, pl.ds(r0, DEC_SEQ), lanes].astype(BF16)
            o_sc[pl.ds(r0, DEC_SEQ), cols] = (
                o_sc[pl.ds(r0, DEC_SEQ), cols] + _dot(qe, s_pad.astype(BF16)))
            ke = ke_sc[half, pl.ds(r0, DEC_SEQ), lanes].astype(BF16)
            vh = vf_sc[pl.ds(r0, DEC_SEQ), cols].astype(BF16)
            upd = _dot_tn(ke, vh)
            dec = _decay_rows(bl_sc[pl.ds(r0, 1), lanes])
            ks = slice(GLA_DK * half, GLA_DK * (half + 1))
            snew_ref[bi, h] = dec[ks] * s0 + upd[ks]
        return carry

    lax.fori_loop(0, SAMPLE_BT, seq_body, 0)
    gla_out = _gla_finish(o_sc[...], g, gn_ref[...])

    o_ref[...] = _out_proj(x, lru_out, gla_out, w_out_ref, gpost_ref)


def _sample_mixer_call(x_all, weights, h0pad, tailpad, s0, l):
    rows = SAMPLE_BT * DEC_SEQ
    nb = DEC_BATCH // SAMPLE_BT
    base = N_PROMPT // rows
    const = lambda *ix: (lambda i: ix)
    st_spec = pl.BlockSpec((None, SAMPLE_BT, GLA_HEADS, GLA_DK, GLA_DV),
                           lambda i: (l, i, 0, 0, 0))
    out_shapes = (
        jax.ShapeDtypeStruct((N_ROWS, D_MODEL), F32),
        jax.ShapeDtypeStruct((N_SAMPLE, D_LRU), F32),
        jax.ShapeDtypeStruct((N_SAMPLE, D_LRU), F32),
        jax.ShapeDtypeStruct((DEC_BATCH, GLA_HEADS, GLA_DK, GLA_DV), F32),
    )
    out_specs = (
        pl.BlockSpec((rows, D_MODEL), lambda i: (base + i, 0)),
        pl.BlockSpec((rows, D_LRU), lambda i: (i, 0)),
        pl.BlockSpec((rows, D_LRU), lambda i: (i, 0)),
        pl.BlockSpec((SAMPLE_BT, GLA_HEADS, GLA_DK, GLA_DV), lambda i: (i, 0, 0, 0)),
    )
    nlev = 4
    scratch = [
        pltpu.VMEM((nlev, 2, rows, D_QK), BF16),
        pltpu.VMEM((nlev, rows, D_QK), BF16),
        pltpu.VMEM((2, rows, D_QK), F32),
        pltpu.VMEM((2, rows, D_QK), F32),
        pltpu.VMEM((rows, D_QK), F32),
        pltpu.VMEM((rows, D_GLA), BF16),
        pltpu.VMEM((rows, D_GLA), F32),
        pltpu.VMEM((rows, D_GLA), F32),
    ]
    return pl.pallas_call(
        _sample_mixer_kernel,
        grid=(nb,),
        in_specs=[pl.BlockSpec((rows, D_MODEL), lambda i: (base + i, 0))]
                 + _weight_specs(l, const)
                 + [pl.BlockSpec((None, rows, D_LRU), lambda i: (l, i, 0)),
                    pl.BlockSpec((None, rows, D_LRU), lambda i: (l, i, 0)),
                    st_spec],
        out_specs=out_specs,
        out_shape=out_shapes,
        scratch_shapes=scratch,
        input_output_aliases={0: 0},
        compiler_params=pltpu.CompilerParams(
            dimension_semantics=("parallel",), vmem_limit_bytes=VMEM_LIMIT),
        name="mixer_sample",
    )(x_all, *weights, h0pad, tailpad, s0)


def _blockdiag(w):
    per = MXU_DIM // LRU_BLOCK
    w = w.reshape(DEPTH, N_LRU_BLOCKS // per, per, LRU_BLOCK, LRU_BLOCK)
    eye = jnp.eye(per, dtype=w.dtype)
    bd = jnp.einsum('lpiab,ij->lpiajb', w, eye)
    return bd.reshape(DEPTH, N_LRU_BLOCKS // per, MXU_DIM, MXU_DIM)


def kernel(x_prompt, x_sample, state_lru_h, state_lru_conv, state_gla, norm_pre, norm_post, w_ffn_gate, w_ffn_up, w_ffn_down, w_in, conv_w, conv_b, lru_wa, lru_ba, lru_wx, lru_bx, lru_lambda, lru_norm, gla_w_gate, gla_b_gate, gla_norm, w_out):
    x_all = jnp.concatenate(
        [x_prompt.reshape(N_PROMPT, D_MODEL), x_sample.reshape(N_SAMPLE, D_MODEL)], axis=0)

    gpre4 = norm_pre.reshape(DEPTH, 3, 1, D_MODEL)
    gpost4 = norm_post.reshape(DEPTH, 3, 1, D_MODEL)
    wg = w_ffn_gate.astype(BF16)
    wu = w_ffn_up.astype(BF16)
    wd = w_ffn_down.astype(BF16)
    w_in_b = w_in.astype(BF16)
    c1 = 2 * D_LRU
    c2 = c1 + 2 * D_QK
    c3 = c2 + D_GLA
    c4 = c3 + D_GLA
    vec = jnp.stack([conv_b, lru_ba, lru_bx, lru_lambda, lru_norm], axis=1)
    weights = (
        gpre4, gpost4,
        w_in_b[:, :, :c1], w_in_b[:, :, c1:c2], w_in_b[:, :, c2:c3], w_in_b[:, :, c3:c4],
        w_in_b[:, :, c4:],
        conv_w, vec,
        _blockdiag(lru_wa).astype(BF16), _blockdiag(lru_wx).astype(BF16),
        gla_w_gate.astype(BF16), gla_b_gate.reshape(DEPTH, 1, D_QK),
        gla_norm.reshape(DEPTH, 1, GLA_DV),
        w_out.astype(BF16),
    )
    h0pad = jnp.pad(state_lru_h[:, :, None, :],
                    ((0, 0), (0, 0), (0, DEC_SEQ - 1), (0, 0))).reshape(DEPTH, N_SAMPLE, D_LRU)
    tailpad = jnp.pad(state_lru_conv,
                      ((0, 0), (0, 0), (SUBLANES - (CONV_W - 1), 0), (0, 0))
                      ).reshape(DEPTH, N_SAMPLE, D_LRU)

    hp, cp, sp, hs, cs, ss = [], [], [], [], [], []
    for l in range(DEPTH):
        x_all = _ffn_call(x_all, gpre4, gpost4, wg, wu, wd, l, 0, "a")
        x_all, hl, ct, sf = _prompt_mixer_call(x_all, weights, l)
        hp.append(hl[:, SUBLANES - 1])
        cp.append(ct[:, SUBLANES - (CONV_W - 1):])
        sp.append(sf)
        x_all, hseq, xl, sn = _sample_mixer_call(x_all, weights, h0pad, tailpad, state_gla, l)
        hs.append(hseq.reshape(DEC_BATCH, DEC_SEQ, D_LRU)[:, DEC_SEQ - 1])
        cs.append(xl.reshape(DEC_BATCH, DEC_SEQ, D_LRU)[:, DEC_SEQ - (CONV_W - 1):])
        ss.append(sn)
        x_all = _ffn_call(x_all, gpre4, gpost4, wg, wu, wd, l, 1, "b")

    yp = x_all[:N_PROMPT].reshape(BATCH, SEQ, D_MODEL)
    ys = x_all[N_PROMPT:].reshape(DEC_BATCH, DEC_SEQ, D_MODEL)
    return (yp, ys, jnp.stack(hp), jnp.stack(cp), jnp.stack(sp),
            jnp.stack(hs), jnp.stack(cs), jnp.stack(ss))
```

```python
import jax
import jax.numpy as jnp
from jax import lax
from jax.experimental import pallas as pl
from jax.experimental.pallas import tpu as pltpu

D_MODEL = 1024
BATCH = 8
SEQ = 2048
DEPTH = 4
DEC_BATCH = 128
DEC_SEQ = 8
D_LRU = 512
N_LRU_BLOCKS = 8
LRU_BLOCK = 64
CONV_W = 4
LRU_C = 8.0
D_GLA = 512
GLA_HEADS = 4
GLA_DV = 128
GLA_DK = 64
GLA_RANK = 16
GLA_NORMALIZER = 16.0
GLA_CHUNK = 64
D_FF = 2816
EPS = 1e-6
D_QK = GLA_HEADS * GLA_DK

N_PROMPT = BATCH * SEQ
N_SAMPLE = DEC_BATCH * DEC_SEQ
N_ROWS = N_PROMPT + N_SAMPLE

SUBLANES = 8
LANES = 128
MXU_DIM = 256
VMEM_LIMIT = 56 * 1024 * 1024

FFN_TM = 512
FFN_FC = 256
PROMPT_TT = 512
SAMPLE_BT = 32
BLK = 64

F32 = jnp.float32
BF16 = jnp.bfloat16


def _rms(x, g):
    ms = jnp.mean(x * x, axis=-1, keepdims=True)
    return x * lax.rsqrt(ms + EPS) * g


def _softplus(x):
    return jnp.maximum(x, 0.0) + jnp.log(1.0 + jnp.exp(-jnp.abs(x)))


def _dot(a, b):
    return jnp.dot(a, b, preferred_element_type=F32)


def _bdot(spec, a, b):
    return jnp.einsum(spec, a, b, preferred_element_type=F32)


def _ffn_kernel(x_ref, gpre_ref, gpost_ref, wg_ref, wu_ref, wd_ref, o_ref):
    x = x_ref[...]
    h = _rms(x, gpre_ref[...]).astype(BF16)
    acc = jnp.zeros((x.shape[0], D_MODEL), F32)
    for c0 in range(0, D_FF, FFN_FC):
        g = _dot(h, wg_ref[:, c0:c0 + FFN_FC])
        u = _dot(h, wu_ref[:, c0:c0 + FFN_FC])
        act = (jax.nn.silu(g) * u).astype(BF16)
        acc = acc + _dot(act, wd_ref[c0:c0 + FFN_FC, :])
    o_ref[...] = x + 0.5 * _rms(acc, gpost_ref[...])


def _ffn_call(x_all, gpre4, gpost4, wg, wu, wd, l, idx, sub):
    const = lambda *ix: (lambda i: ix)
    nidx = 2 * idx
    return pl.pallas_call(
        _ffn_kernel,
        grid=(N_ROWS // FFN_TM,),
        in_specs=[
            pl.BlockSpec((FFN_TM, D_MODEL), lambda i: (i, 0)),
            pl.BlockSpec((None, None, 1, D_MODEL), const(l, nidx, 0, 0)),
            pl.BlockSpec((None, None, 1, D_MODEL), const(l, nidx, 0, 0)),
            pl.BlockSpec((None, None, D_MODEL, D_FF), const(l, idx, 0, 0),
                         pipeline_mode=pl.Buffered(1)),
            pl.BlockSpec((None, None, D_MODEL, D_FF), const(l, idx, 0, 0),
                         pipeline_mode=pl.Buffered(1)),
            pl.BlockSpec((None, None, D_FF, D_MODEL), const(l, idx, 0, 0),
                         pipeline_mode=pl.Buffered(1)),
        ],
        out_specs=pl.BlockSpec((FFN_TM, D_MODEL), lambda i: (i, 0)),
        out_shape=jax.ShapeDtypeStruct((N_ROWS, D_MODEL), F32),
        input_output_aliases={0: 0},
        compiler_params=pltpu.CompilerParams(
            dimension_semantics=("parallel",), vmem_limit_bytes=VMEM_LIMIT),
        name=f"ffn_{sub}",
    )(x_all, gpre4, gpost4, wg, wu, wd)


def _grouped(x):
    return x.reshape(x.shape[0] // SUBLANES, SUBLANES, x.shape[1])


def _sub_iota(x3):
    return lax.broadcasted_iota(jnp.int32, x3.shape, 1)


def _groups_down(x3, q, first=None):
    if q == 0:
        return x3
    head = jnp.zeros((q,) + x3.shape[1:], x3.dtype) if first is None else first
    return jnp.concatenate([head, x3[:x3.shape[0] - q]], axis=0)


def _groups_up(x3, q):
    if q == 0:
        return x3
    return jnp.concatenate([x3[q:], jnp.zeros((q,) + x3.shape[1:], x3.dtype)], axis=0)


def _shift_down(x, k, first=None, in_group=False):
    x3 = _grouped(x)
    q, r = divmod(k, SUBLANES)
    if r == 0:
        y = _groups_down(x3, q)
    else:
        xr = pltpu.roll(x3, r, 1)
        if in_group:
            y = xr
        else:
            fr = None if first is None else pltpu.roll(_grouped(first), r, 1)
            y = jnp.where(_sub_iota(x3) < r, _groups_down(xr, q + 1, fr if q == 0 else None),
                          _groups_down(xr, q))
    return y.reshape(x.shape)


def _shift_up(x, k, in_group=False):
    x3 = _grouped(x)
    q, r = divmod(k, SUBLANES)
    if r == 0:
        y = _groups_up(x3, q)
    else:
        xr = pltpu.roll(x3, SUBLANES - r, 1)
        if in_group:
            y = xr
        else:
            y = jnp.where(_sub_iota(x3) >= SUBLANES - r, _groups_up(xr, q + 1), _groups_up(xr, q))
    return y.reshape(x.shape)


def _row_in_seg(shape, seg):
    return lax.broadcasted_iota(jnp.int32, shape, 0) & (seg - 1)


def _scan_affine8(a, b):
    t = _row_in_seg(a.shape, SUBLANES)
    d = 1
    while d < SUBLANES:
        m = t >= d
        b = jnp.where(m, a * _shift_down(b, d, in_group=True) + b, b)
        a = jnp.where(m, a * _shift_down(a, d, in_group=True), a)
        d *= 2
    return a, b


def _seg_cumsum(x, seg):
    t = _row_in_seg(x.shape, seg)
    d = 1
    while d < seg:
        x = jnp.where(t >= d, x + _shift_down(x, d, in_group=(seg == SUBLANES)), x)
        d *= 2
    return x


def _causal_conv(xl, convw_ref, bias, shifted):
    xc = None
    for jj in range(CONV_W):
        kk = CONV_W - 1 - jj
        term = (xl if kk == 0 else shifted(kk)) * convw_ref[jj:jj + 1, :]
        xc = term if xc is None else xc + term
    return xc + bias


def _lru_gates(xc, vec_ref, wa_ref, wx_ref):
    half = D_LRU // 2
    xb = xc.astype(BF16)
    xb0, xb1 = xb[:, :half], xb[:, half:]
    ba, bx, lam = vec_ref[1:2, :], vec_ref[2:3, :], vec_ref[3:4, :]
    r = jax.nn.sigmoid(
        jnp.concatenate([_dot(xb0, wa_ref[0]), _dot(xb1, wa_ref[1])], axis=1) + ba)
    i = jax.nn.sigmoid(
        jnp.concatenate([_dot(xb0, wx_ref[0]), _dot(xb1, wx_ref[1])], axis=1) + bx)
    log_a = -LRU_C * r * _softplus(-lam)
    a = jnp.exp(log_a)
    mult = jnp.sqrt(-jnp.tanh(log_a) * (a * a + 1.0))
    return a, mult, i


def _level_masks(nlev):
    t = lax.broadcasted_iota(jnp.int32, (1, BLK, BLK), 1)
    s = lax.broadcasted_iota(jnp.int32, (1, BLK, BLK), 2)
    masks = [t == s]
    for lv in range(nlev):
        masks.append((((t >> lv) ^ (s >> lv)) == 1) & (((t >> lv) & 1) == 1))
    return masks


def _head_halves(val, dtype):
    second = (lax.broadcasted_iota(jnp.int32, val.shape, 1) & (LANES - 1)) >= GLA_DK
    zero = jnp.zeros_like(val)
    return (jnp.where(second, zero, val).astype(dtype), jnp.where(second, val, zero).astype(dtype))


def _gla_operands(q, k, gk, seg, state_dtype):
    t = lax.broadcasted_iota(jnp.int32, gk.shape, 0)
    b = _seg_cumsum(gk, seg)
    qs = [_head_halves(q, BF16)]
    ks = [k.astype(BF16)]
    ref = b - gk
    end = b
    s = 1
    while s < seg:
        qs.append(_head_halves(q * jnp.exp(b - ref), BF16))
        ks.append((k * jnp.exp(end - b)).astype(BF16))
        bit = (t & s) != 0
        ref = jnp.where(bit, _shift_down(ref, s, in_group=True), ref)
        end = jnp.where(bit, end, _shift_up(end, s, in_group=True))
        s *= 2
    qe = _head_halves(q * jnp.exp(b), state_dtype)
    ke = _head_halves(k * jnp.exp(end - b), state_dtype)
    return qs, ks, qe, ke, end


def _blocks(x, lanes):
    return x[:, lanes].reshape(x.shape[0] // BLK, BLK, LANES)


def _scores(qs, ks, masks, h):
    grp, half = divmod(h, 2)
    lanes = slice(LANES * grp, LANES * (grp + 1))
    p = None
    for lv, m in enumerate(masks):
        sc = _bdot('ctd,csd->cts', _blocks(qs[lv][half], lanes), _blocks(ks[lv], lanes))
        p = jnp.where(m, sc, 0.0 if p is None else p)
    return p.astype(BF16)


def _decay_rows(bl_row):
    return jnp.exp(jnp.transpose(jnp.broadcast_to(bl_row, (LANES, LANES))))


def _gla_finish(o, g, gn):
    outs = []
    for h in range(GLA_HEADS):
        sl = slice(GLA_DV * h, GLA_DV * (h + 1))
        outs.append(_rms(o[:, sl], gn) * jax.nn.silu(g[:, sl]))
    return jnp.concatenate(outs, axis=1)


def _project(x, gpre_ref, w_lru_ref, w_qk_ref, w_v_ref, w_g_ref, w_gd_ref):
    hn = _rms(x, gpre_ref[...]).astype(BF16)
    xy = _dot(hn, w_lru_ref[...])
    qk = _dot(hn, w_qk_ref[...])
    v = _dot(hn, w_v_ref[...])
    g = _dot(hn, w_g_ref[...])
    gd = _dot(hn, w_gd_ref[...])
    return xy[:, :D_LRU], xy[:, D_LRU:], qk[:, :D_QK] * (GLA_DK ** -0.5), qk[:, D_QK:], v, g, gd


def _gk_from(gd, wgate_ref, bgate_ref):
    z = _dot(gd.astype(BF16), wgate_ref[...]) + bgate_ref[...]
    return -_softplus(-z) * (1.0 / GLA_NORMALIZER)


def _out_proj(x, lru_out, gla_out, w_out_ref, gpost_ref):
    m = (_dot(lru_out.astype(BF16), w_out_ref[:D_LRU, :])
         + _dot(gla_out.astype(BF16), w_out_ref[D_LRU:, :]))
    return x + _rms(m, gpost_ref[...])


def _prompt_mixer_kernel(
        x_ref, gpre_ref, gpost_ref, w_lru_ref, w_qk_ref, w_v_ref, w_g_ref, w_gd_ref,
        convw_ref, vec_ref, wa_ref, wx_ref, wgate_ref, bgate_ref, gn_ref, w_out_ref,
        o_ref, hlast_ref, ctail_ref, sfin_ref,
        tail_sc, h_sc, s_sc):
    j = pl.program_id(1)
    tt = PROMPT_TT
    nchunk = tt // BLK

    @pl.when(j == 0)
    def _():
        tail_sc[...] = jnp.zeros_like(tail_sc)
        h_sc[...] = jnp.zeros_like(h_sc)
        s_sc[...] = jnp.zeros_like(s_sc)

    x = x_ref[...]
    xl, yl, q, k, v, g, gd = _project(
        x, gpre_ref, w_lru_ref, w_qk_ref, w_v_ref, w_g_ref, w_gd_ref)

    tail = tail_sc[...]
    xc = _causal_conv(xl, convw_ref, vec_ref[0:1, :], lambda kk: _shift_down(xl, kk, first=tail))
    tail_sc[...] = xl[tt - SUBLANES:, :]
    ctail_ref[...] = xl[tt - SUBLANES:, :]

    a, mult, gi = _lru_gates(xc, vec_ref, wa_ref, wx_ref)
    first_row = (lax.broadcasted_iota(jnp.int32, a.shape, 0) == 0) & (j == 0)
    mult = jnp.where(first_row, 1.0, mult)
    a8, h8 = _scan_affine8(a, mult * gi * xc)
    hin = h_sc[SUBLANES - 1:SUBLANES, :]
    hs = []
    for gidx in range(tt // SUBLANES):
        rows = slice(SUBLANES * gidx, SUBLANES * (gidx + 1))
        hg = a8[rows] * hin + h8[rows]
        hin = hg[SUBLANES - 1:SUBLANES, :]
        hs.append(hg)
    hseq = jnp.concatenate(hs, axis=0)
    h_sc[...] = hs[-1]
    hlast_ref[...] = hs[-1]
    lru_out = _rms(hseq * jax.nn.gelu(yl), vec_ref[4:5, :])

    gk = _gk_from(gd, wgate_ref, bgate_ref)
    qs, ks, qe, ke, bl = _gla_operands(q, k, gk, GLA_CHUNK, BF16)
    masks = _level_masks(6)
    vb = v.astype(BF16)
    decs = [[_decay_rows(bl[c * BLK:c * BLK + 1, LANES * grp:LANES * (grp + 1)])
             for c in range(nchunk)] for grp in range(GLA_HEADS // 2)]
    outs = []
    for h in range(GLA_HEADS):
        grp, half = divmod(h, 2)
        lanes = slice(LANES * grp, LANES * (grp + 1))
        p = _scores(qs, ks, masks, h)
        vh = vb[:, GLA_DV * h:GLA_DV * (h + 1)].reshape(nchunk, BLK, GLA_DV)
        upd = _bdot('csd,csv->cdv', _blocks(ke[half], lanes), vh)
        s = s_sc[h]
        states = []
        for c in range(nchunk):
            states.append(s.astype(BF16))
            s = decs[grp][c] * s + upd[c]
        s_sc[h] = s
        sfin_ref[h] = s[GLA_DK * half:GLA_DK * (half + 1), :]
        o = (_bdot('cts,csv->ctv', p, vh)
             + _bdot('ctd,cdv->ctv', _blocks(qe[half], lanes), jnp.stack(states)))
        outs.append(o.reshape(tt, GLA_DV))
    gla_out = _gla_finish(jnp.concatenate(outs, axis=1), g, gn_ref[...])

    o_ref[...] = _out_proj(x, lru_out, gla_out, w_out_ref, gpost_ref)


def _weight_specs(l, const):
    d = D_MODEL
    return [
        pl.BlockSpec((None, None, 1, d), const(l, 1, 0, 0)),
        pl.BlockSpec((None, None, 1, d), const(l, 1, 0, 0)),
        pl.BlockSpec((None, d, 2 * D_LRU), const(l, 0, 0)),
        pl.BlockSpec((None, d, 2 * D_QK), const(l, 0, 0)),
        pl.BlockSpec((None, d, D_GLA), const(l, 0, 0)),
        pl.BlockSpec((None, d, D_GLA), const(l, 0, 0)),
        pl.BlockSpec((None, d, GLA_RANK), const(l, 0, 0)),
        pl.BlockSpec((None, CONV_W, D_LRU), const(l, 0, 0)),
        pl.BlockSpec((None, 5, D_LRU), const(l, 0, 0)),
        pl.BlockSpec((None, 2, MXU_DIM, MXU_DIM), const(l, 0, 0, 0)),
        pl.BlockSpec((None, 2, MXU_DIM, MXU_DIM), const(l, 0, 0, 0)),
        pl.BlockSpec((None, GLA_RANK, D_QK), const(l, 0, 0)),
        pl.BlockSpec((None, 1, D_QK), const(l, 0, 0)),
        pl.BlockSpec((None, 1, GLA_DV), const(l, 0, 0)),
        pl.BlockSpec((None, d, d), const(l, 0, 0)),
    ]


def _prompt_mixer_call(x_all, weights, l):
    tt = PROMPT_TT
    nt = SEQ // tt
    const = lambda *ix: (lambda b, j: ix)
    out_shapes = (
        jax.ShapeDtypeStruct((N_ROWS, D_MODEL), F32),
        jax.ShapeDtypeStruct((BATCH, SUBLANES, D_LRU), F32),
        jax.ShapeDtypeStruct((BATCH, SUBLANES, D_LRU), F32),
        jax.ShapeDtypeStruct((BATCH, GLA_HEADS, GLA_DK, GLA_DV), F32),
    )
    out_specs = (
        pl.BlockSpec((tt, D_MODEL), lambda b, j: (b * nt + j, 0)),
        pl.BlockSpec((None, SUBLANES, D_LRU), lambda b, j: (b, 0, 0)),
        pl.BlockSpec((None, SUBLANES, D_LRU), lambda b, j: (b, 0, 0)),
        pl.BlockSpec((None, GLA_HEADS, GLA_DK, GLA_DV), lambda b, j: (b, 0, 0, 0)),
    )
    scratch = [
        pltpu.VMEM((SUBLANES, D_LRU), F32),
        pltpu.VMEM((SUBLANES, D_LRU), F32),
        pltpu.VMEM((GLA_HEADS, LANES, GLA_DV), F32),
    ]
    return pl.pallas_call(
        _prompt_mixer_kernel,
        grid=(BATCH, nt),
        in_specs=[pl.BlockSpec((tt, D_MODEL), lambda b, j: (b * nt + j, 0))]
                 + _weight_specs(l, const),
        out_specs=out_specs,
        out_shape=out_shapes,
        scratch_shapes=scratch,
        input_output_aliases={0: 0},
        compiler_params=pltpu.CompilerParams(
            dimension_semantics=("parallel", "arbitrary"), vmem_limit_bytes=VMEM_LIMIT),
        name="mixer_prompt",
    )(x_all, *weights)


def _sample_mixer_kernel(
        x_ref, gpre_ref, gpost_ref, w_lru_ref, w_qk_ref, w_v_ref, w_g_ref, w_gd_ref,
        convw_ref, vec_ref, wa_ref, wx_ref, wgate_ref, bgate_ref, gn_ref, w_out_ref,
        h0_ref, tail_ref, s0_ref,
        o_ref, hseq_ref, xl_ref, snew_ref):
    rows = SAMPLE_BT * DEC_SEQ
    nseq = SAMPLE_BT
    x = x_ref[...]
    xl, yl, q, k, v, g, gd = _project(
        x, gpre_ref, w_lru_ref, w_qk_ref, w_v_ref, w_g_ref, w_gd_ref)

    t8 = _row_in_seg(xl.shape, DEC_SEQ)
    tail = tail_ref[...]
    xc = _causal_conv(
        xl, convw_ref, vec_ref[0:1, :],
        lambda kk: jnp.where(t8 < kk, _shift_down(tail, kk, in_group=True),
                             _shift_down(xl, kk, in_group=True)))
    xl_ref[...] = xl

    a, mult, gi = _lru_gates(xc, vec_ref, wa_ref, wx_ref)
    bterm = mult * gi * xc + a * h0_ref[...]
    _, hseq = _scan_affine8(a, bterm)
    hseq_ref[...] = hseq
    lru_out = _rms(hseq * jax.nn.gelu(yl), vec_ref[4:5, :])

    gk = _gk_from(gd, wgate_ref, bgate_ref)
    qs, ks, qe, ke, bl = _gla_operands(q, k, gk, DEC_SEQ, F32)
    masks = _level_masks(3)
    vb = v.astype(BF16)
    zeros_half = jnp.zeros((nseq, GLA_DK, GLA_DV), F32)

    def per_seq(x2):
        return x2.reshape(nseq, DEC_SEQ, x2.shape[1]).astype(BF16)

    decs = []
    for grp in range(GLA_HEADS // 2):
        bl3 = bl[:, LANES * grp:LANES * (grp + 1)].reshape(nseq, DEC_SEQ, LANES)
        decs.append([_decay_rows(bl3[b, 0:1, :]) for b in range(nseq)])
    outs = []
    for h in range(GLA_HEADS):
        grp, half = divmod(h, 2)
        lanes = slice(LANES * grp, LANES * (grp + 1))
        cols = slice(GLA_DV * h, GLA_DV * (h + 1))
        ks_rows = slice(GLA_DK * half, GLA_DK * (half + 1))
        p = _scores(qs, ks, masks, h)
        vh = vb[:, cols].reshape(rows // BLK, BLK, GLA_DV)
        o_intra = _bdot('cts,csv->ctv', p, vh).reshape(rows, GLA_DV)
        s0 = s0_ref[:, h]
        s_pad = jnp.concatenate([zeros_half, s0] if half else [s0, zeros_half], axis=1)
        o_inter = _bdot('btd,bdv->btv', per_seq(qe[half][:, lanes]), s_pad.astype(BF16))
        outs.append(o_intra + o_inter.reshape(rows, GLA_DV))
        upd = _bdot('bsd,bsv->bdv', per_seq(ke[half][:, lanes]), per_seq(v[:, cols]))
        for b in range(nseq):
            snew_ref[b, h] = decs[grp][b][ks_rows] * s0[b] + upd[b, ks_rows, :]
    gla_out = _gla_finish(jnp.concatenate(outs, axis=1), g, gn_ref[...])

    o_ref[...] = _out_proj(x, lru_out, gla_out, w_out_ref, gpost_ref)


def _sample_mixer_call(x_all, weights, h0pad, tailpad, s0, l):
    rows = SAMPLE_BT * DEC_SEQ
    nb = DEC_BATCH // SAMPLE_BT
    base = N_PROMPT // rows
    const = lambda *ix: (lambda i: ix)
    st_spec = pl.BlockSpec((None, SAMPLE_BT, GLA_HEADS, GLA_DK, GLA_DV),
                           lambda i: (l, i, 0, 0, 0))
    out_shapes = (
        jax.ShapeDtypeStruct((N_ROWS, D_MODEL), F32),
        jax.ShapeDtypeStruct((N_SAMPLE, D_LRU), F32),
        jax.ShapeDtypeStruct((N_SAMPLE, D_LRU), F32),
        jax.ShapeDtypeStruct((DEC_BATCH, GLA_HEADS, GLA_DK, GLA_DV), F32),
    )
    out_specs = (
        pl.BlockSpec((rows, D_MODEL), lambda i: (base + i, 0)),
        pl.BlockSpec((rows, D_LRU), lambda i: (i, 0)),
        pl.BlockSpec((rows, D_LRU), lambda i: (i, 0)),
        pl.BlockSpec((SAMPLE_BT, GLA_HEADS, GLA_DK, GLA_DV), lambda i: (i, 0, 0, 0)),
    )
    return pl.pallas_call(
        _sample_mixer_kernel,
        grid=(nb,),
        in_specs=[pl.BlockSpec((rows, D_MODEL), lambda i: (base + i, 0))]
                 + _weight_specs(l, const)
                 + [pl.BlockSpec((None, rows, D_LRU), lambda i: (l, i, 0)),
                    pl.BlockSpec((None, rows, D_LRU), lambda i: (l, i, 0)),
                    st_spec],
        out_specs=out_specs,
        out_shape=out_shapes,
        input_output_aliases={0: 0},
        compiler_params=pltpu.CompilerParams(
            dimension_semantics=("parallel",), vmem_limit_bytes=VMEM_LIMIT),
        name="mixer_sample",
    )(x_all, *weights, h0pad, tailpad, s0)


def _blockdiag(w):
    per = MXU_DIM // LRU_BLOCK
    w = w.reshape(DEPTH, N_LRU_BLOCKS // per, per, LRU_BLOCK, LRU_BLOCK)
    eye = jnp.eye(per, dtype=w.dtype)
    bd = jnp.einsum('lpiab,ij->lpiajb', w, eye)
    return bd.reshape(DEPTH, N_LRU_BLOCKS // per, MXU_DIM, MXU_DIM)


def kernel(x_prompt, x_sample, state_lru_h, state_lru_conv, state_gla, norm_pre, norm_post, w_ffn_gate, w_ffn_up, w_ffn_down, w_in, conv_w, conv_b, lru_wa, lru_ba, lru_wx, lru_bx, lru_lambda, lru_norm, gla_w_gate, gla_b_gate, gla_norm, w_out):
    x_all = jnp.concatenate(
        [x_prompt.reshape(N_PROMPT, D_MODEL), x_sample.reshape(N_SAMPLE, D_MODEL)], axis=0)

    gpre4 = norm_pre.reshape(DEPTH, 3, 1, D_MODEL)
    gpost4 = norm_post.reshape(DEPTH, 3, 1, D_MODEL)
    wg = w_ffn_gate.astype(BF16)
    wu = w_ffn_up.astype(BF16)
    wd = w_ffn_down.astype(BF16)
    w_in_b = w_in.astype(BF16)
    c1 = 2 * D_LRU
    c2 = c1 + 2 * D_QK
    c3 = c2 + D_GLA
    c4 = c3 + D_GLA
    vec = jnp.stack([conv_b, lru_ba, lru_bx, lru_lambda, lru_norm], axis=1)
    weights = (
        gpre4, gpost4,
        w_in_b[:, :, :c1], w_in_b[:, :, c1:c2], w_in_b[:, :, c2:c3], w_in_b[:, :, c3:c4],
        w_in_b[:, :, c4:],
        conv_w, vec,
        _blockdiag(lru_wa).astype(BF16), _blockdiag(lru_wx).astype(BF16),
        gla_w_gate.astype(BF16), gla_b_gate.reshape(DEPTH, 1, D_QK),
        gla_norm.reshape(DEPTH, 1, GLA_DV),
        w_out.astype(BF16),
    )
    h0pad = jnp.pad(state_lru_h[:, :, None, :],
                    ((0, 0), (0, 0), (0, DEC_SEQ - 1), (0, 0))).reshape(DEPTH, N_SAMPLE, D_LRU)
    tailpad = jnp.pad(state_lru_conv,
                      ((0, 0), (0, 0), (SUBLANES - (CONV_W - 1), 0), (0, 0))
                      ).reshape(DEPTH, N_SAMPLE, D_LRU)

    hp, cp, sp, hs, cs, ss = [], [], [], [], [], []
    for l in range(DEPTH):
        x_all = _ffn_call(x_all, gpre4, gpost4, wg, wu, wd, l, 0, "a")
        x_all, hl, ct, sf = _prompt_mixer_call(x_all, weights, l)
        hp.append(hl[:, SUBLANES - 1])
        cp.append(ct[:, SUBLANES - (CONV_W - 1):])
        sp.append(sf)
        x_all, hseq, xl, sn = _sample_mixer_call(x_all, weights, h0pad, tailpad, state_gla, l)
        hs.append(hseq.reshape(DEC_BATCH, DEC_SEQ, D_LRU)[:, DEC_SEQ - 1])
        cs.append(xl.reshape(DEC_BATCH, DEC_SEQ, D_LRU)[:, DEC_SEQ - (CONV_W - 1):])
        ss.append(sn)
        x_all = _ffn_call(x_all, gpre4, gpost4, wg, wu, wd, l, 1, "b")

    yp = x_all[:N_PROMPT].reshape(BATCH, SEQ, D_MODEL)
    ys = x_all[N_PROMPT:].reshape(DEC_BATCH, DEC_SEQ, D_MODEL)
    return (yp, ys, jnp.stack(hp), jnp.stack(cp), jnp.stack(sp),
            jnp.stack(hs), jnp.stack(cs), jnp.stack(ss))
```

```python
import functools

import jax
import jax.numpy as jnp
from jax import lax
from jax.experimental import pallas as pl
from jax.experimental.pallas import tpu as pltpu

D_MODEL = 1024
BATCH = 8
SEQ = 2048
DEPTH = 4
DEC_BATCH = 128
DEC_SEQ = 8
D_LRU = 512
N_LRU_BLOCKS = 8
LRU_BLOCK = 64
CONV_W = 4
LRU_C = 8.0
D_GLA = 512
GLA_HEADS = 4
GLA_DV = 128
GLA_DK = 64
GLA_RANK = 16
GLA_NORMALIZER = 16.0
GLA_CHUNK = 64
D_FF = 2816
EPS = 1e-6
D_QK = GLA_HEADS * GLA_DK

N_PROMPT = BATCH * SEQ
N_SAMPLE = DEC_BATCH * DEC_SEQ
N_ROWS = N_PROMPT + N_SAMPLE

SUBLANES = 8
LANES = 128
MXU_DIM = 256
VMEM_LIMIT = 56 * 1024 * 1024

FFN_TM = 512
FFN_FC = 256
PROMPT_TT = 512
SAMPLE_BT = 32
BLK = 64

F32 = jnp.float32
BF16 = jnp.bfloat16


def _rms(x, g):
    ms = jnp.mean(x * x, axis=-1, keepdims=True)
    return x * lax.rsqrt(ms + EPS) * g


def _softplus(x):
    return jnp.maximum(x, 0.0) + jnp.log(1.0 + jnp.exp(-jnp.abs(x)))


def _dot(a, b):
    return jnp.dot(a, b, preferred_element_type=F32)


def _bdot(spec, a, b):
    return jnp.einsum(spec, a, b, preferred_element_type=F32)


NP_TILES = N_PROMPT // FFN_TM
N_FC = D_FF // FFN_FC


def _ffn_kernel(*refs, first, last, convert):
    it = iter(refs)
    x_refs = (next(it), next(it)) if first else (next(it),)
    gpre_ref, gpost_ref, wg_ref, wu_ref, wd_ref = (next(it) for _ in range(5))
    nxt = tuple(next(it) for _ in range(3)) if convert else ()
    o_refs = (next(it), next(it)) if last else (next(it),)
    cvt = tuple(next(it) for _ in range(3)) if convert else ()
    i = pl.program_id(0)

    if first:
        x_sc = next(it)

        @pl.when(i < NP_TILES)
        def _():
            x_sc[...] = x_refs[0][...]

        @pl.when(i >= NP_TILES)
        def _():
            x_sc[...] = x_refs[1][...]

        x = x_sc[...]
    else:
        x = x_refs[0][...]

    h = _rms(x, gpre_ref[...]).astype(BF16)
    acc = jnp.zeros((x.shape[0], D_MODEL), F32)
    for c0 in range(0, D_FF, FFN_FC):
        g = _dot(h, wg_ref[:, c0:c0 + FFN_FC])
        u = _dot(h, wu_ref[:, c0:c0 + FFN_FC])
        act = (jax.nn.silu(g) * u).astype(BF16)
        acc = acc + _dot(act, wd_ref[c0:c0 + FFN_FC, :])
    out = x + 0.5 * _rms(acc, gpost_ref[...])

    if last:
        @pl.when(i < NP_TILES)
        def _():
            o_refs[0][...] = out

        @pl.when(i >= NP_TILES)
        def _():
            o_refs[1][...] = out
    else:
        o_refs[0][...] = out

    if convert:
        @pl.when(i < N_FC)
        def _():
            for src, dst in zip(nxt, cvt):
                dst[...] = src[...].astype(BF16)


def _ffn_call(xs, gpre4, gpost4, w_bf16, w_f32_next, l, idx, nxt_l_idx, sub, first, last):
    convert = w_f32_next is not None
    const = lambda *ix: (lambda i: ix)
    nidx = 2 * idx
    prompt_ix = lambda i: (jnp.minimum(i, NP_TILES - 1), 0)
    sample_ix = lambda i: (jnp.maximum(i - NP_TILES, 0), 0)
    row_spec = lambda ix: pl.BlockSpec((FFN_TM, D_MODEL), ix)
    chunk = lambda i: jnp.minimum(i, N_FC - 1)

    in_specs = [row_spec(prompt_ix), row_spec(sample_ix)] if first else [row_spec(lambda i: (i, 0))]
    in_specs += [
        pl.BlockSpec((None, None, 1, D_MODEL), const(l, nidx, 0, 0)),
        pl.BlockSpec((None, None, 1, D_MODEL), const(l, nidx, 0, 0)),
        pl.BlockSpec((D_MODEL, D_FF), const(0, 0), pipeline_mode=pl.Buffered(1)),
        pl.BlockSpec((D_MODEL, D_FF), const(0, 0), pipeline_mode=pl.Buffered(1)),
        pl.BlockSpec((D_FF, D_MODEL), const(0, 0), pipeline_mode=pl.Buffered(1)),
    ]
    args = list(xs) + [gpre4, gpost4] + list(w_bf16)
    if convert:
        nl, ni = nxt_l_idx
        in_specs += [
            pl.BlockSpec((None, None, D_MODEL, FFN_FC), lambda i: (nl, ni, 0, chunk(i))),
            pl.BlockSpec((None, None, D_MODEL, FFN_FC), lambda i: (nl, ni, 0, chunk(i))),
            pl.BlockSpec((None, None, FFN_FC, D_MODEL), lambda i: (nl, ni, chunk(i), 0)),
        ]
        args += list(w_f32_next)

    if last:
        out_specs = [row_spec(prompt_ix), row_spec(sample_ix)]
        out_shape = [jax.ShapeDtypeStruct((N_PROMPT, D_MODEL), F32),
                     jax.ShapeDtypeStruct((N_SAMPLE, D_MODEL), F32)]
    else:
        out_specs = [row_spec(lambda i: (i, 0))]
        out_shape = [jax.ShapeDtypeStruct((N_ROWS, D_MODEL), F32)]
    n_act = len(out_specs)
    if convert:
        out_specs += [
            pl.BlockSpec((D_MODEL, FFN_FC), lambda i: (0, chunk(i))),
            pl.BlockSpec((D_MODEL, FFN_FC), lambda i: (0, chunk(i))),
            pl.BlockSpec((FFN_FC, D_MODEL), lambda i: (chunk(i), 0)),
        ]
        out_shape += [jax.ShapeDtypeStruct((D_MODEL, D_FF), BF16),
                      jax.ShapeDtypeStruct((D_MODEL, D_FF), BF16),
                      jax.ShapeDtypeStruct((D_FF, D_MODEL), BF16)]

    res = pl.pallas_call(
        functools.partial(_ffn_kernel, first=first, last=last, convert=convert),
        grid=(N_ROWS // FFN_TM,),
        in_specs=in_specs,
        out_specs=out_specs,
        out_shape=out_shape,
        scratch_shapes=[pltpu.VMEM((FFN_TM, D_MODEL), F32)] if first else [],
        input_output_aliases={} if (first or last) else {0: 0},
        compiler_params=pltpu.CompilerParams(
            dimension_semantics=("arbitrary",), vmem_limit_bytes=VMEM_LIMIT),
        name=f"ffn_{sub}",
    )(*args)
    return res[:n_act], tuple(res[n_act:])


def _grouped(x):
    return x.reshape(x.shape[0] // SUBLANES, SUBLANES, x.shape[1])


def _sub_iota(x3):
    return lax.broadcasted_iota(jnp.int32, x3.shape, 1)


def _groups_down(x3, q, first=None):
    if q == 0:
        return x3
    head = jnp.zeros((q,) + x3.shape[1:], x3.dtype) if first is None else first
    return jnp.concatenate([head, x3[:x3.shape[0] - q]], axis=0)


def _groups_up(x3, q):
    if q == 0:
        return x3
    return jnp.concatenate([x3[q:], jnp.zeros((q,) + x3.shape[1:], x3.dtype)], axis=0)


def _shift_down(x, k, first=None, in_group=False):
    x3 = _grouped(x)
    q, r = divmod(k, SUBLANES)
    if r == 0:
        y = _groups_down(x3, q)
    else:
        xr = pltpu.roll(x3, r, 1)
        if in_group:
            y = xr
        else:
            fr = None if first is None else pltpu.roll(_grouped(first), r, 1)
            y = jnp.where(_sub_iota(x3) < r, _groups_down(xr, q + 1, fr if q == 0 else None),
                          _groups_down(xr, q))
    return y.reshape(x.shape)


def _shift_up(x, k, in_group=False):
    x3 = _grouped(x)
    q, r = divmod(k, SUBLANES)
    if r == 0:
        y = _groups_up(x3, q)
    else:
        xr = pltpu.roll(x3, SUBLANES - r, 1)
        if in_group:
            y = xr
        else:
            y = jnp.where(_sub_iota(x3) >= SUBLANES - r, _groups_up(xr, q + 1), _groups_up(xr, q))
    return y.reshape(x.shape)


def _row_in_seg(shape, seg):
    return lax.broadcasted_iota(jnp.int32, shape, 0) & (seg - 1)


def _scan_affine8(a, b):
    t = _row_in_seg(a.shape, SUBLANES)
    d = 1
    while d < SUBLANES:
        m = t >= d
        b = jnp.where(m, a * _shift_down(b, d, in_group=True) + b, b)
        a = jnp.where(m, a * _shift_down(a, d, in_group=True), a)
        d *= 2
    return a, b


def _seg_cumsum(x, seg):
    t = _row_in_seg(x.shape, seg)
    d = 1
    while d < seg:
        x = jnp.where(t >= d, x + _shift_down(x, d, in_group=(seg == SUBLANES)), x)
        d *= 2
    return x


def _causal_conv(xl, convw_ref, bias, shifted):
    xc = None
    for jj in range(CONV_W):
        kk = CONV_W - 1 - jj
        term = (xl if kk == 0 else shifted(kk)) * convw_ref[jj:jj + 1, :]
        xc = term if xc is None else xc + term
    return xc + bias


def _lru_gates(xc, vec_ref, wa_ref, wx_ref):
    half = D_LRU // 2
    xb = xc.astype(BF16)
    xb0, xb1 = xb[:, :half], xb[:, half:]
    ba, bx, lam = vec_ref[1:2, :], vec_ref[2:3, :], vec_ref[3:4, :]
    r = jax.nn.sigmoid(
        jnp.concatenate([_dot(xb0, wa_ref[0]), _dot(xb1, wa_ref[1])], axis=1) + ba)
    i = jax.nn.sigmoid(
        jnp.concatenate([_dot(xb0, wx_ref[0]), _dot(xb1, wx_ref[1])], axis=1) + bx)
    log_a = -LRU_C * r * _softplus(-lam)
    a = jnp.exp(log_a)
    mult = jnp.sqrt(-jnp.tanh(log_a) * (a * a + 1.0))
    return a, mult, i


def _lru_carry(a8, h8, hin):
    hs = []
    for gidx in range(a8.shape[0] // SUBLANES):
        rows = slice(SUBLANES * gidx, SUBLANES * (gidx + 1))
        hg = a8[rows] * hin + h8[rows]
        hin = hg[SUBLANES - 1:SUBLANES, :]
        hs.append(hg)
    return hs


def _level_masks(nlev):
    t = lax.broadcasted_iota(jnp.int32, (1, BLK, BLK), 1)
    s = lax.broadcasted_iota(jnp.int32, (1, BLK, BLK), 2)
    masks = [t == s]
    for lv in range(nlev):
        masks.append((((t >> lv) ^ (s >> lv)) == 1) & (((t >> lv) & 1) == 1))
    return masks


def _head_halves(val, dtype):
    second = (lax.broadcasted_iota(jnp.int32, val.shape, 1) & (LANES - 1)) >= GLA_DK
    zero = jnp.zeros_like(val)
    return (jnp.where(second, zero, val).astype(dtype), jnp.where(second, val, zero).astype(dtype))


def _gla_operands(q, k, gk, seg, state_dtype):
    t = lax.broadcasted_iota(jnp.int32, gk.shape, 0)
    b = _seg_cumsum(gk, seg)
    qs = [_head_halves(q, BF16)]
    ks = [k.astype(BF16)]
    ref = b - gk
    end = b
    s = 1
    while s < seg:
        qs.append(_head_halves(q * jnp.exp(b - ref), BF16))
        ks.append((k * jnp.exp(end - b)).astype(BF16))
        bit = (t & s) != 0
        ref = jnp.where(bit, _shift_down(ref, s, in_group=True), ref)
        end = jnp.where(bit, end, _shift_up(end, s, in_group=True))
        s *= 2
    qe = _head_halves(q * jnp.exp(b), state_dtype)
    ke = _head_halves(k * jnp.exp(end - b), state_dtype)
    return qs, ks, qe, ke, end


def _blocks(x, lanes):
    return x[:, lanes].reshape(x.shape[0] // BLK, BLK, LANES)


def _scores(qs, ks, masks, h):
    grp, half = divmod(h, 2)
    lanes = slice(LANES * grp, LANES * (grp + 1))
    p = None
    for lv, m in enumerate(masks):
        sc = _bdot('ctd,csd->cts', _blocks(qs[lv][half], lanes), _blocks(ks[lv], lanes))
        p = jnp.where(m, sc, 0.0 if p is None else p)
    return p.astype(BF16)


def _decay_rows(bl_row):
    return jnp.exp(jnp.transpose(jnp.broadcast_to(bl_row, (LANES, LANES))))


def _gla_finish(o, g, gn):
    outs = []
    for h in range(GLA_HEADS):
        sl = slice(GLA_DV * h, GLA_DV * (h + 1))
        outs.append(_rms(o[:, sl], gn) * jax.nn.silu(g[:, sl]))
    return jnp.concatenate(outs, axis=1)


def _gk_from(gd, wgate_ref, bgate_ref):
    z = _dot(gd.astype(BF16), wgate_ref[...]) + bgate_ref[...]
    return -_softplus(-z) * (1.0 / GLA_NORMALIZER)


def _out_proj(x, lru_out, gla_out, w_out_ref, gpost_ref):
    m = (_dot(lru_out.astype(BF16), w_out_ref[:D_LRU, :])
         + _dot(gla_out.astype(BF16), w_out_ref[D_LRU:, :]))
    return x + _rms(m, gpost_ref[...])


def _prompt_mixer_kernel(
        x_ref, gpre_ref, gpost_ref, w_lru_ref, w_qk_ref, w_v_ref, w_g_ref, w_gd_ref,
        convw_ref, vec_ref, wa_ref, wx_ref, wgate_ref, bgate_ref, gn_ref, w_out_ref,
        o_ref, hlast_ref, ctail_ref, sfin_ref,
        tail_sc, h_sc, s_sc, x_sc, xy_sc, qk_sc, gd_sc, v_sc, g_sc):
    s = pl.program_id(0)
    tt = PROMPT_TT
    nt = SEQ // tt
    nchunk = tt // BLK
    seq_start = lax.rem(s + nt - 1, nt) == 0

    @pl.when(s == 0)
    def _():
        for ref in (x_sc, xy_sc, qk_sc, gd_sc, v_sc, g_sc):
            ref[...] = jnp.zeros_like(ref)

    @pl.when((s == 0) | seq_start)
    def _():
        tail_sc[...] = jnp.zeros_like(tail_sc)
        h_sc[...] = jnp.zeros_like(h_sc)
        s_sc[...] = jnp.zeros_like(s_sc)

    x_new = x_ref[...]
    hn = _rms(x_new, gpre_ref[...]).astype(BF16)

    xy = xy_sc[...]
    xl, yl = xy[:, :D_LRU], xy[:, D_LRU:]
    tail = tail_sc[...]
    xc = _causal_conv(xl, convw_ref, vec_ref[0:1, :], lambda kk: _shift_down(xl, kk, first=tail))
    tail_sc[...] = xl[tt - SUBLANES:, :]
    ctail_ref[...] = xl[tt - SUBLANES:, :]
    xy_sc[...] = _dot(hn, w_lru_ref[...])
    a, mult, gi = _lru_gates(xc, vec_ref, wa_ref, wx_ref)

    qk = qk_sc[...]
    q, k = qk[:, :D_QK] * (GLA_DK ** -0.5), qk[:, D_QK:]
    gk = _gk_from(gd_sc[...], wgate_ref, bgate_ref)
    qs, ks, qe, ke, bl = _gla_operands(q, k, gk, GLA_CHUNK, BF16)
    qk_sc[...] = _dot(hn, w_qk_ref[...])
    gd_sc[...] = _dot(hn, w_gd_ref[...])

    first_row = (lax.broadcasted_iota(jnp.int32, a.shape, 0) == 0) & seq_start
    mult = jnp.where(first_row, 1.0, mult)
    a8, h8 = _scan_affine8(a, mult * gi * xc)
    hs = _lru_carry(a8, h8, h_sc[SUBLANES - 1:SUBLANES, :])
    h_sc[...] = hs[-1]
    hlast_ref[...] = hs[-1]
    lru_out = _rms(jnp.concatenate(hs, axis=0) * jax.nn.gelu(yl), vec_ref[4:5, :])

    masks = _level_masks(6)
    vb = v_sc[...]
    ps, vhs, upds = [], [], []
    for h in range(GLA_HEADS):
        grp, half = divmod(h, 2)
        lanes = slice(LANES * grp, LANES * (grp + 1))
        ps.append(_scores(qs, ks, masks, h))
        vhs.append(vb[:, GLA_DV * h:GLA_DV * (h + 1)].reshape(nchunk, BLK, GLA_DV))
        upds.append(_bdot('csd,csv->cdv', _blocks(ke[half], lanes), vhs[h]))
    v_sc[...] = _dot(hn, w_v_ref[...]).astype(BF16)
    g_new = _dot(hn, w_g_ref[...])

    decs = [[_decay_rows(bl[c * BLK:c * BLK + 1, LANES * grp:LANES * (grp + 1)])
             for c in range(nchunk)] for grp in range(GLA_HEADS // 2)]
    outs = []
    for h in range(GLA_HEADS):
        grp, half = divmod(h, 2)
        lanes = slice(LANES * grp, LANES * (grp + 1))
        st = s_sc[h]
        states = []
        for c in range(nchunk):
            states.append(st.astype(BF16))
            st = decs[grp][c] * st + upds[h][c]
        s_sc[h] = st
        sfin_ref[h] = st[GLA_DK * half:GLA_DK * (half + 1), :]
        o = (_bdot('cts,csv->ctv', ps[h], vhs[h])
             + _bdot('ctd,cdv->ctv', _blocks(qe[half], lanes), jnp.stack(states)))
        outs.append(o.reshape(tt, GLA_DV))
    gla_out = _gla_finish(jnp.concatenate(outs, axis=1), g_sc[...], gn_ref[...])
    g_sc[...] = g_new

    o_ref[...] = _out_proj(x_sc[...], lru_out, gla_out, w_out_ref, gpost_ref)
    x_sc[...] = x_new


def _weight_specs(l, const):
    d = D_MODEL
    return [
        pl.BlockSpec((None, None, 1, d), const(l, 1, 0, 0)),
        pl.BlockSpec((None, None, 1, d), const(l, 1, 0, 0)),
        pl.BlockSpec((None, d, 2 * D_LRU), const(l, 0, 0)),
        pl.BlockSpec((None, d, 2 * D_QK), const(l, 0, 0)),
        pl.BlockSpec((None, d, D_GLA), const(l, 0, 0)),
        pl.BlockSpec((None, d, D_GLA), const(l, 0, 0)),
        pl.BlockSpec((None, d, GLA_RANK), const(l, 0, 0)),
        pl.BlockSpec((None, CONV_W, D_LRU), const(l, 0, 0)),
        pl.BlockSpec((None, 5, D_LRU), const(l, 0, 0)),
        pl.BlockSpec((None, 2, MXU_DIM, MXU_DIM), const(l, 0, 0, 0)),
        pl.BlockSpec((None, 2, MXU_DIM, MXU_DIM), const(l, 0, 0, 0)),
        pl.BlockSpec((None, GLA_RANK, D_QK), const(l, 0, 0)),
        pl.BlockSpec((None, 1, D_QK), const(l, 0, 0)),
        pl.BlockSpec((None, 1, GLA_DV), const(l, 0, 0)),
        pl.BlockSpec((None, d, d), const(l, 0, 0)),
    ]


def _prompt_mixer_call(x_all, weights, l):
    tt = PROMPT_TT
    nt = SEQ // tt
    ntiles = BATCH * nt
    const = lambda *ix: (lambda s: ix)
    done = lambda s: jnp.maximum(s - 1, 0)
    out_shapes = (
        jax.ShapeDtypeStruct((N_ROWS, D_MODEL), F32),
        jax.ShapeDtypeStruct((BATCH, SUBLANES, D_LRU), F32),
        jax.ShapeDtypeStruct((BATCH, SUBLANES, D_LRU), F32),
        jax.ShapeDtypeStruct((BATCH, GLA_HEADS, GLA_DK, GLA_DV), F32),
    )
    out_specs = (
        pl.BlockSpec((tt, D_MODEL), lambda s: (done(s), 0)),
        pl.BlockSpec((None, SUBLANES, D_LRU), lambda s: (done(s) // nt, 0, 0)),
        pl.BlockSpec((None, SUBLANES, D_LRU), lambda s: (done(s) // nt, 0, 0)),
        pl.BlockSpec((None, GLA_HEADS, GLA_DK, GLA_DV), lambda s: (done(s) // nt, 0, 0, 0)),
    )
    scratch = [
        pltpu.VMEM((SUBLANES, D_LRU), F32),
        pltpu.VMEM((SUBLANES, D_LRU), F32),
        pltpu.VMEM((GLA_HEADS, LANES, GLA_DV), F32),
        pltpu.VMEM((tt, D_MODEL), F32),
        pltpu.VMEM((tt, 2 * D_LRU), F32),
        pltpu.VMEM((tt, 2 * D_QK), F32),
        pltpu.VMEM((tt, GLA_RANK), F32),
        pltpu.VMEM((tt, D_GLA), BF16),
        pltpu.VMEM((tt, D_GLA), F32),
    ]
    return pl.pallas_call(
        _prompt_mixer_kernel,
        grid=(ntiles + 1,),
        in_specs=[pl.BlockSpec((tt, D_MODEL), lambda s: (jnp.minimum(s, ntiles - 1), 0))]
                 + _weight_specs(l, const),
        out_specs=out_specs,
        out_shape=out_shapes,
        scratch_shapes=scratch,
        input_output_aliases={0: 0},
        compiler_params=pltpu.CompilerParams(
            dimension_semantics=("arbitrary",), vmem_limit_bytes=VMEM_LIMIT),
        name="mixer_prompt",
    )(x_all, *weights)


def _sample_mixer_kernel(
        x_ref, gpre_ref, gpost_ref, w_lru_ref, w_qk_ref, w_v_ref, w_g_ref, w_gd_ref,
        convw_ref, vec_ref, wa_ref, wx_ref, wgate_ref, bgate_ref, gn_ref, w_out_ref,
        h0_ref, tail_ref, s0_ref, *rest, carried):
    o_ref, hseq_ref, xl_ref, snew_ref = rest[1:] if carried else rest
    rows = SAMPLE_BT * DEC_SEQ
    nseq = SAMPLE_BT
    x = x_ref[...]
    hn = _rms(x, gpre_ref[...]).astype(BF16)
    xy = _dot(hn, w_lru_ref[...])
    xl, yl = xy[:, :D_LRU], xy[:, D_LRU:]
    qk = _dot(hn, w_qk_ref[...])
    q, k = qk[:, :D_QK] * (GLA_DK ** -0.5), qk[:, D_QK:]
    v = _dot(hn, w_v_ref[...])
    g = _dot(hn, w_g_ref[...])
    gd = _dot(hn, w_gd_ref[...])

    t8 = _row_in_seg(xl.shape, DEC_SEQ)
    tail = tail_ref[...]
    xc = _causal_conv(
        xl, convw_ref, vec_ref[0:1, :],
        lambda kk: jnp.where(t8 < kk, _shift_down(tail, kk, in_group=True),
                             _shift_down(xl, kk, in_group=True)))
    xl_ref[...] = xl

    a, mult, gi = _lru_gates(xc, vec_ref, wa_ref, wx_ref)
    bterm = mult * gi * xc + a * h0_ref[...]
    _, hseq = _scan_affine8(a, bterm)
    hseq_ref[...] = hseq
    lru_out = _rms(hseq * jax.nn.gelu(yl), vec_ref[4:5, :])

    gk = _gk_from(gd, wgate_ref, bgate_ref)
    qs, ks, qe, ke, bl = _gla_operands(q, k, gk, DEC_SEQ, F32)
    masks = _level_masks(3)
    vb = v.astype(BF16)
    zeros_half = jnp.zeros((nseq, GLA_DK, GLA_DV), F32)

    def per_seq(x2):
        return x2.reshape(nseq, DEC_SEQ, x2.shape[1]).astype(BF16)

    decs = []
    for grp in range(GLA_HEADS // 2):
        bl3 = bl[:, LANES * grp:LANES * (grp + 1)].reshape(nseq, DEC_SEQ, LANES)
        decs.append([_decay_rows(bl3[b, 0:1, :]) for b in range(nseq)])
    outs = []
    for h in range(GLA_HEADS):
        grp, half = divmod(h, 2)
        lanes = slice(LANES * grp, LANES * (grp + 1))
        cols = slice(GLA_DV * h, GLA_DV * (h + 1))
        ks_rows = slice(GLA_DK * half, GLA_DK * (half + 1))
        p = _scores(qs, ks, masks, h)
        vh = vb[:, cols].reshape(rows // BLK, BLK, GLA_DV)
        o_intra = _bdot('cts,csv->ctv', p, vh).reshape(rows, GLA_DV)
        s0 = s0_ref[:, h]
        s_pad = jnp.concatenate([zeros_half, s0] if half else [s0, zeros_half], axis=1)
        o_inter = _bdot('btd,bdv->btv', per_seq(qe[half][:, lanes]), s_pad.astype(BF16))
        outs.append(o_intra + o_inter.reshape(rows, GLA_DV))
        upd = _bdot('bsd,bsv->bdv', per_seq(ke[half][:, lanes]), per_seq(v[:, cols]))
        for b in range(nseq):
            snew_ref[b, h] = decs[grp][b][ks_rows] * s0[b] + upd[b, ks_rows, :]
    gla_out = _gla_finish(jnp.concatenate(outs, axis=1), g, gn_ref[...])

    o_ref[...] = _out_proj(x, lru_out, gla_out, w_out_ref, gpost_ref)


def _sample_mixer_call(x_all, weights, h0pad, tailpad, s0, snew_all, l):
    rows = SAMPLE_BT * DEC_SEQ
    nb = DEC_BATCH // SAMPLE_BT
    base = N_PROMPT // rows
    const = lambda *ix: (lambda i: ix)
    st_spec = pl.BlockSpec((None, SAMPLE_BT, GLA_HEADS, GLA_DK, GLA_DV),
                           lambda i: (l, i, 0, 0, 0))
    out_shapes = (
        jax.ShapeDtypeStruct((N_ROWS, D_MODEL), F32),
        jax.ShapeDtypeStruct((N_SAMPLE, D_LRU), F32),
        jax.ShapeDtypeStruct((N_SAMPLE, D_LRU), F32),
        jax.ShapeDtypeStruct((DEPTH, DEC_BATCH, GLA_HEADS, GLA_DK, GLA_DV), F32),
    )
    out_specs = (
        pl.BlockSpec((rows, D_MODEL), lambda i: (base + i, 0)),
        pl.BlockSpec((rows, D_LRU), lambda i: (i, 0)),
        pl.BlockSpec((rows, D_LRU), lambda i: (i, 0)),
        st_spec,
    )
    in_specs = ([pl.BlockSpec((rows, D_MODEL), lambda i: (base + i, 0))]
                + _weight_specs(l, const)
                + [pl.BlockSpec((None, rows, D_LRU), lambda i: (l, i, 0)),
                   pl.BlockSpec((None, rows, D_LRU), lambda i: (l, i, 0)),
                   st_spec])
    args = [x_all, *weights, h0pad, tailpad, s0]
    aliases = {0: 0}
    if snew_all is not None:
        aliases[len(args)] = 3
        in_specs.append(pl.BlockSpec(memory_space=pl.ANY))
        args.append(snew_all)
    return pl.pallas_call(
        functools.partial(_sample_mixer_kernel, carried=snew_all is not None),
        grid=(nb,),
        in_specs=in_specs,
        out_specs=out_specs,
        out_shape=out_shapes,
        input_output_aliases=aliases,
        compiler_params=pltpu.CompilerParams(
            dimension_semantics=("parallel",), vmem_limit_bytes=VMEM_LIMIT),
        name="mixer_sample",
    )(*args)


def _blockdiag(w):
    per = MXU_DIM // LRU_BLOCK
    w = w.reshape(DEPTH, N_LRU_BLOCKS // per, per, LRU_BLOCK, LRU_BLOCK)
    eye = jnp.eye(per, dtype=w.dtype)
    bd = jnp.einsum('lpiab,ij->lpiajb', w, eye)
    return bd.reshape(DEPTH, N_LRU_BLOCKS // per, MXU_DIM, MXU_DIM)


def kernel(x_prompt, x_sample, state_lru_h, state_lru_conv, state_gla, norm_pre, norm_post, w_ffn_gate, w_ffn_up, w_ffn_down, w_in, conv_w, conv_b, lru_wa, lru_ba, lru_wx, lru_bx, lru_lambda, lru_norm, gla_w_gate, gla_b_gate, gla_norm, w_out):
    gpre4 = norm_pre.reshape(DEPTH, 3, 1, D_MODEL)
    gpost4 = norm_post.reshape(DEPTH, 3, 1, D_MODEL)
    w_ffn_f32 = (w_ffn_gate, w_ffn_up, w_ffn_down)
    c1 = 2 * D_LRU
    c2 = c1 + 2 * D_QK
    c3 = c2 + D_GLA
    c4 = c3 + D_GLA
    vec = jnp.stack([conv_b, lru_ba, lru_bx, lru_lambda, lru_norm], axis=1)
    weights = (
        gpre4, gpost4,
        w_in[:, :, :c1].astype(BF16), w_in[:, :, c1:c2].astype(BF16),
        w_in[:, :, c2:c3].astype(BF16), w_in[:, :, c3:c4].astype(BF16),
        w_in[:, :, c4:].astype(BF16),
        conv_w, vec,
        _blockdiag(lru_wa).astype(BF16), _blockdiag(lru_wx).astype(BF16),
        gla_w_gate.astype(BF16), gla_b_gate.reshape(DEPTH, 1, D_QK),
        gla_norm.reshape(DEPTH, 1, GLA_DV),
        w_out.astype(BF16),
    )
    h0pad = jnp.pad(state_lru_h[:, :, None, :],
                    ((0, 0), (0, 0), (0, DEC_SEQ - 1), (0, 0))).reshape(DEPTH, N_SAMPLE, D_LRU)
    tailpad = jnp.pad(state_lru_conv,
                      ((0, 0), (0, 0), (SUBLANES - (CONV_W - 1), 0), (0, 0))
                      ).reshape(DEPTH, N_SAMPLE, D_LRU)

    w_bf16 = tuple(w[0, 0].astype(BF16) for w in w_ffn_f32)
    ffn_order = [(l, idx) for l in range(DEPTH) for idx in range(2)]

    def ffn(xs, l, idx, first=False, last=False):
        pos = ffn_order.index((l, idx))
        nxt = ffn_order[pos + 1] if pos + 1 < len(ffn_order) else None
        return _ffn_call(xs, gpre4, gpost4, w_bf16, w_ffn_f32 if nxt else None, l, idx, nxt,
                         "ab"[idx], first, last)

    hp, cp, sp, hs, cs = [], [], [], [], []
    snew_all = None
    xs = (x_prompt.reshape(N_PROMPT, D_MODEL), x_sample.reshape(N_SAMPLE, D_MODEL))
    for l in range(DEPTH):
        (x_all,), w_bf16 = ffn(xs, l, 0, first=(l == 0))
        x_all, hl, ct, sf = _prompt_mixer_call(x_all, weights, l)
        hp.append(hl[:, SUBLANES - 1])
        cp.append(ct[:, SUBLANES - (CONV_W - 1):])
        sp.append(sf)
        x_all, hseq, xl, snew_all = _sample_mixer_call(
            x_all, weights, h0pad, tailpad, state_gla, snew_all, l)
        hs.append(hseq.reshape(DEC_BATCH, DEC_SEQ, D_LRU)[:, DEC_SEQ - 1])
        cs.append(xl.reshape(DEC_BATCH, DEC_SEQ, D_LRU)[:, DEC_SEQ - (CONV_W - 1):])
        xs, w_bf16 = ffn((x_all,), l, 1, last=(l == DEPTH - 1))

    yp = xs[0].reshape(BATCH, SEQ, D_MODEL)
    ys = xs[1].reshape(DEC_BATCH, DEC_SEQ, D_MODEL)
    return (yp, ys, jnp.stack(hp), jnp.stack(cp), jnp.stack(sp),
            jnp.stack(hs), jnp.stack(cs), snew_all)
```

```python
import functools

import jax
import jax.numpy as jnp
from jax import lax
from jax.experimental import pallas as pl
from jax.experimental.pallas import tpu as pltpu

D_MODEL = 1024
BATCH = 8
SEQ = 2048
DEPTH = 4
DEC_BATCH = 128
DEC_SEQ = 8
D_LRU = 512
N_LRU_BLOCKS = 8
LRU_BLOCK = 64
CONV_W = 4
LRU_C = 8.0
D_GLA = 512
GLA_HEADS = 4
GLA_DV = 128
GLA_DK = 64
GLA_RANK = 16
GLA_NORMALIZER = 16.0
GLA_CHUNK = 64
D_FF = 2816
EPS = 1e-6
D_QK = GLA_HEADS * GLA_DK

N_PROMPT = BATCH * SEQ
N_SAMPLE = DEC_BATCH * DEC_SEQ
N_ROWS = N_PROMPT + N_SAMPLE

SUBLANES = 8
LANES = 128
MXU_DIM = 256
VMEM_LIMIT = 56 * 1024 * 1024

FFN_TM = 1024
FFN_TM_FIRST = 512
FFN_FC = 256
PROMPT_TT = 512
SAMPLE_BT = 32
BLK = 64
GLA_FAST_LIMIT = 40.0

F32 = jnp.float32
BF16 = jnp.bfloat16


def _rms(x, g):
    ms = jnp.mean(x * x, axis=-1, keepdims=True)
    return x * lax.rsqrt(ms + EPS) * g


def _softplus(x):
    return jnp.maximum(x, 0.0) + jnp.log(1.0 + jnp.exp(-jnp.abs(x)))


def _dot(a, b):
    return jnp.dot(a, b, preferred_element_type=F32)


def _bdot(spec, a, b):
    return jnp.einsum(spec, a, b, preferred_element_type=F32)


N_FC = D_FF // FFN_FC


def _ffn_kernel(*refs, first, last, convert, np_tiles):
    it = iter(refs)
    x_refs = (next(it), next(it)) if first else (next(it),)
    gpre_ref, gpost_ref, wg_ref, wu_ref, wd_ref = (next(it) for _ in range(5))
    nxt = tuple(next(it) for _ in range(3)) if convert else ()
    o_refs = (next(it), next(it)) if last else (next(it),)
    cvt = tuple(next(it) for _ in range(3)) if convert else ()
    i = pl.program_id(0)

    if first:
        x_sc = next(it)

        @pl.when(i < np_tiles)
        def _():
            x_sc[...] = x_refs[0][...]

        @pl.when(i >= np_tiles)
        def _():
            x_sc[...] = x_refs[1][...]

        x = x_sc[...]
    else:
        x = x_refs[0][...]

    h = _rms(x, gpre_ref[...]).astype(BF16)
    acc = jnp.zeros((x.shape[0], D_MODEL), F32)
    for c0 in range(0, D_FF, FFN_FC):
        g = _dot(h, wg_ref[:, c0:c0 + FFN_FC])
        u = _dot(h, wu_ref[:, c0:c0 + FFN_FC])
        act = (jax.nn.silu(g) * u).astype(BF16)
        acc = acc + _dot(act, wd_ref[c0:c0 + FFN_FC, :])
    out = x + 0.5 * _rms(acc, gpost_ref[...])

    if last:
        @pl.when(i < np_tiles)
        def _():
            o_refs[0][...] = out

        @pl.when(i >= np_tiles)
        def _():
            o_refs[1][...] = out
    else:
        o_refs[0][...] = out

    if convert:
        @pl.when(i < N_FC)
        def _():
            for src, dst in zip(nxt, cvt):
                dst[...] = src[...].astype(BF16)


def _ffn_call(xs, gpre4, gpost4, w_bf16, w_f32_next, l, idx, nxt_l_idx, sub, first, last):
    convert = w_f32_next is not None
    const = lambda *ix: (lambda i: ix)
    nidx = 2 * idx
    tm = FFN_TM_FIRST if first else FFN_TM
    np_tiles = N_PROMPT // tm
    prompt_ix = lambda i: (jnp.minimum(i, np_tiles - 1), 0)
    sample_ix = lambda i: (jnp.maximum(i - np_tiles, 0), 0)
    row_spec = lambda ix: pl.BlockSpec((tm, D_MODEL), ix)
    chunk = lambda i: jnp.minimum(i, N_FC - 1)

    in_specs = [row_spec(prompt_ix), row_spec(sample_ix)] if first else [row_spec(lambda i: (i, 0))]
    in_specs += [
        pl.BlockSpec((None, None, 1, D_MODEL), const(l, nidx, 0, 0)),
        pl.BlockSpec((None, None, 1, D_MODEL), const(l, nidx, 0, 0)),
        pl.BlockSpec((D_MODEL, D_FF), const(0, 0), pipeline_mode=pl.Buffered(1)),
        pl.BlockSpec((D_MODEL, D_FF), const(0, 0), pipeline_mode=pl.Buffered(1)),
        pl.BlockSpec((D_FF, D_MODEL), const(0, 0), pipeline_mode=pl.Buffered(1)),
    ]
    args = list(xs) + [gpre4, gpost4] + list(w_bf16)
    if convert:
        nl, ni = nxt_l_idx
        in_specs += [
            pl.BlockSpec((None, None, D_MODEL, FFN_FC), lambda i: (nl, ni, 0, chunk(i))),
            pl.BlockSpec((None, None, D_MODEL, FFN_FC), lambda i: (nl, ni, 0, chunk(i))),
            pl.BlockSpec((None, None, FFN_FC, D_MODEL), lambda i: (nl, ni, chunk(i), 0)),
        ]
        args += list(w_f32_next)

    if last:
        out_specs = [row_spec(prompt_ix), row_spec(sample_ix)]
        out_shape = [jax.ShapeDtypeStruct((N_PROMPT, D_MODEL), F32),
                     jax.ShapeDtypeStruct((N_SAMPLE, D_MODEL), F32)]
    else:
        out_specs = [row_spec(lambda i: (i, 0))]
        out_shape = [jax.ShapeDtypeStruct((N_ROWS, D_MODEL), F32)]
    n_act = len(out_specs)
    if convert:
        out_specs += [
            pl.BlockSpec((D_MODEL, FFN_FC), lambda i: (0, chunk(i))),
            pl.BlockSpec((D_MODEL, FFN_FC), lambda i: (0, chunk(i))),
            pl.BlockSpec((FFN_FC, D_MODEL), lambda i: (chunk(i), 0)),
        ]
        out_shape += [jax.ShapeDtypeStruct((D_MODEL, D_FF), BF16),
                      jax.ShapeDtypeStruct((D_MODEL, D_FF), BF16),
                      jax.ShapeDtypeStruct((D_FF, D_MODEL), BF16)]

    res = pl.pallas_call(
        functools.partial(_ffn_kernel, first=first, last=last, convert=convert,
                          np_tiles=np_tiles),
        grid=(N_ROWS // tm,),
        in_specs=in_specs,
        out_specs=out_specs,
        out_shape=out_shape,
        scratch_shapes=[pltpu.VMEM((tm, D_MODEL), F32)] if first else [],
        input_output_aliases={} if (first or last) else {0: 0},
        compiler_params=pltpu.CompilerParams(
            dimension_semantics=("arbitrary",), vmem_limit_bytes=VMEM_LIMIT),
        name=f"ffn_{sub}",
    )(*args)
    return res[:n_act], tuple(res[n_act:])


def _grouped(x):
    return x.reshape(x.shape[0] // SUBLANES, SUBLANES, x.shape[1])


def _sub_iota(x3):
    return lax.broadcasted_iota(jnp.int32, x3.shape, 1)


def _groups_down(x3, q, first=None):
    if q == 0:
        return x3
    head = jnp.zeros((q,) + x3.shape[1:], x3.dtype) if first is None else first
    return jnp.concatenate([head, x3[:x3.shape[0] - q]], axis=0)


def _groups_up(x3, q):
    if q == 0:
        return x3
    return jnp.concatenate([x3[q:], jnp.zeros((q,) + x3.shape[1:], x3.dtype)], axis=0)


def _shift_down(x, k, first=None, in_group=False):
    x3 = _grouped(x)
    q, r = divmod(k, SUBLANES)
    if r == 0:
        y = _groups_down(x3, q)
    else:
        xr = pltpu.roll(x3, r, 1)
        if in_group:
            y = xr
        else:
            fr = None if first is None else pltpu.roll(_grouped(first), r, 1)
            y = jnp.where(_sub_iota(x3) < r, _groups_down(xr, q + 1, fr if q == 0 else None),
                          _groups_down(xr, q))
    return y.reshape(x.shape)


def _shift_up(x, k, in_group=False):
    x3 = _grouped(x)
    q, r = divmod(k, SUBLANES)
    if r == 0:
        y = _groups_up(x3, q)
    else:
        xr = pltpu.roll(x3, SUBLANES - r, 1)
        if in_group:
            y = xr
        else:
            y = jnp.where(_sub_iota(x3) >= SUBLANES - r, _groups_up(xr, q + 1), _groups_up(xr, q))
    return y.reshape(x.shape)


def _row_in_seg(shape, seg):
    return lax.broadcasted_iota(jnp.int32, shape, 0) & (seg - 1)


def _scan_affine8(a, b):
    t = _row_in_seg(a.shape, SUBLANES)
    d = 1
    while d < SUBLANES:
        m = t >= d
        b = jnp.where(m, a * _shift_down(b, d, in_group=True) + b, b)
        a = jnp.where(m, a * _shift_down(a, d, in_group=True), a)
        d *= 2
    return a, b


def _seg_cumsum(x, seg):
    t = _row_in_seg(x.shape, seg)
    d = 1
    while d < seg:
        x = jnp.where(t >= d, x + _shift_down(x, d, in_group=(seg == SUBLANES)), x)
        d *= 2
    return x


def _causal_conv(xl, convw_ref, bias, shifted):
    xc = None
    for jj in range(CONV_W):
        kk = CONV_W - 1 - jj
        term = (xl if kk == 0 else shifted(kk)) * convw_ref[jj:jj + 1, :]
        xc = term if xc is None else xc + term
    return xc + bias


def _lru_gates(xc, vec_ref, wa_ref, wx_ref):
    half = D_LRU // 2
    xb = xc.astype(BF16)
    xb0, xb1 = xb[:, :half], xb[:, half:]
    ba, bx, lam = vec_ref[1:2, :], vec_ref[2:3, :], vec_ref[3:4, :]
    r = jax.nn.sigmoid(
        jnp.concatenate([_dot(xb0, wa_ref[0]), _dot(xb1, wa_ref[1])], axis=1) + ba)
    i = jax.nn.sigmoid(
        jnp.concatenate([_dot(xb0, wx_ref[0]), _dot(xb1, wx_ref[1])], axis=1) + bx)
    log_a = -LRU_C * r * _softplus(-lam)
    a = jnp.exp(log_a)
    mult = jnp.sqrt(-jnp.tanh(log_a) * (a * a + 1.0))
    return a, mult, i


def _lru_carry(a8, h8, hin):
    hs = []
    for gidx in range(a8.shape[0] // SUBLANES):
        rows = slice(SUBLANES * gidx, SUBLANES * (gidx + 1))
        hg = a8[rows] * hin + h8[rows]
        hin = hg[SUBLANES - 1:SUBLANES, :]
        hs.append(hg)
    return hs


def _level_masks(nlev):
    t = lax.broadcasted_iota(jnp.int32, (1, BLK, BLK), 1)
    s = lax.broadcasted_iota(jnp.int32, (1, BLK, BLK), 2)
    masks = [t == s]
    for lv in range(nlev):
        masks.append((((t >> lv) ^ (s >> lv)) == 1) & (((t >> lv) & 1) == 1))
    return masks


def _head_halves(val, dtype):
    second = (lax.broadcasted_iota(jnp.int32, val.shape, 1) & (LANES - 1)) >= GLA_DK
    zero = jnp.zeros_like(val)
    return (jnp.where(second, zero, val).astype(dtype), jnp.where(second, val, zero).astype(dtype))


def _gla_operands(q, k, gk, seg, state_dtype):
    t = lax.broadcasted_iota(jnp.int32, gk.shape, 0)
    b = _seg_cumsum(gk, seg)
    qs = [_head_halves(q, BF16)]
    ks = [k.astype(BF16)]
    ref = b - gk
    end = b
    s = 1
    while s < seg:
        qs.append(_head_halves(q * jnp.exp(b - ref), BF16))
        ks.append((k * jnp.exp(end - b)).astype(BF16))
        bit = (t & s) != 0
        ref = jnp.where(bit, _shift_down(ref, s, in_group=True), ref)
        end = jnp.where(bit, end, _shift_up(end, s, in_group=True))
        s *= 2
    qe = _head_halves(q * jnp.exp(b), state_dtype)
    ke = _head_halves(k * jnp.exp(end - b), state_dtype)
    return qs, ks, qe, ke, end


def _gla_operands_bounded(q, k, gk):
    n, c = gk.shape
    b = _seg_cumsum(gk, GLA_CHUNK)
    b3 = b.reshape(n // GLA_CHUNK, GLA_CHUNK, c)
    total = jnp.broadcast_to(b3[:, GLA_CHUNK - 1:GLA_CHUNK, :], b3.shape).reshape(n, c)
    qe = _head_halves(q * jnp.exp(b), BF16)
    ke = _head_halves(k * jnp.exp(total - b), BF16)
    kf = (k * jnp.exp(-b)).astype(BF16)
    return qe, ke, kf, total


def _chunk_decay_floor(gk):
    n, c = gk.shape
    return jnp.min(jnp.sum(gk.reshape(n // GLA_CHUNK, GLA_CHUNK, c), axis=1))


def _blocks(x, lanes):
    return x[:, lanes].reshape(x.shape[0] // BLK, BLK, LANES)


def _scores_bounded(qe, kf, h):
    grp, half = divmod(h, 2)
    lanes = slice(LANES * grp, LANES * (grp + 1))
    t = lax.broadcasted_iota(jnp.int32, (1, BLK, BLK), 1)
    s = lax.broadcasted_iota(jnp.int32, (1, BLK, BLK), 2)
    sc = _bdot('ctd,csd->cts', _blocks(qe[half], lanes), _blocks(kf, lanes))
    return jnp.where(t >= s, sc, 0.0).astype(BF16)


def _scores(qs, ks, masks, h):
    grp, half = divmod(h, 2)
    lanes = slice(LANES * grp, LANES * (grp + 1))
    p = None
    for lv, m in enumerate(masks):
        sc = _bdot('ctd,csd->cts', _blocks(qs[lv][half], lanes), _blocks(ks[lv], lanes))
        p = jnp.where(m, sc, 0.0 if p is None else p)
    return p.astype(BF16)


def _decay_rows(bl_row):
    return jnp.exp(jnp.transpose(jnp.broadcast_to(bl_row, (LANES, LANES))))


def _gla_finish(o, g, gn):
    outs = []
    for h in range(GLA_HEADS):
        sl = slice(GLA_DV * h, GLA_DV * (h + 1))
        outs.append(_rms(o[:, sl], gn) * jax.nn.silu(g[:, sl]))
    return jnp.concatenate(outs, axis=1)


def _gk_from(gd, wgate_ref, bgate_ref):
    z = _dot(gd.astype(BF16), wgate_ref[...]) + bgate_ref[...]
    return -_softplus(-z) * (1.0 / GLA_NORMALIZER)


def _out_proj(x, lru_out, gla_out, w_out_ref, gpost_ref):
    m = (_dot(lru_out.astype(BF16), w_out_ref[:D_LRU, :])
         + _dot(gla_out.astype(BF16), w_out_ref[D_LRU:, :]))
    return x + _rms(m, gpost_ref[...])


def _prompt_mixer_kernel(
        x_ref, gpre_ref, gpost_ref, w_lru_ref, w_qk_ref, w_v_ref, w_g_ref, w_gd_ref,
        convw_ref, vec_ref, wa_ref, wx_ref, wgate_ref, bgate_ref, gn_ref, w_out_ref,
        o_ref, hlast_ref, ctail_ref, sfin_ref,
        tail_sc, h_sc, s_sc, x_sc, xy_sc, qk_sc, gk_sc, v_sc, g_sc, bounded_sc):
    s = pl.program_id(0)
    tt = PROMPT_TT
    nt = SEQ // tt
    nchunk = tt // BLK
    seq_start = lax.rem(s + nt - 1, nt) == 0

    @pl.when(s == 0)
    def _():
        for ref in (x_sc, xy_sc, qk_sc, gk_sc, v_sc, g_sc):
            ref[...] = jnp.zeros_like(ref)
        bounded_sc[0] = 1

    @pl.when((s == 0) | seq_start)
    def _():
        tail_sc[...] = jnp.zeros_like(tail_sc)
        h_sc[...] = jnp.zeros_like(h_sc)
        s_sc[...] = jnp.zeros_like(s_sc)

    def body(bounded):
        x_new = x_ref[...]
        hn = _rms(x_new, gpre_ref[...]).astype(BF16)

        xy = xy_sc[...]
        xl, yl = xy[:, :D_LRU], xy[:, D_LRU:]
        tail = tail_sc[...]
        xc = _causal_conv(xl, convw_ref, vec_ref[0:1, :],
                          lambda kk: _shift_down(xl, kk, first=tail))
        tail_sc[...] = xl[tt - SUBLANES:, :]
        ctail_ref[...] = xl[tt - SUBLANES:, :]
        xy_sc[...] = _dot(hn, w_lru_ref[...])
        a, mult, gi = _lru_gates(xc, vec_ref, wa_ref, wx_ref)

        qk = qk_sc[...]
        q, k = qk[:, :D_QK] * (GLA_DK ** -0.5), qk[:, D_QK:]
        gk = gk_sc[...]
        if bounded:
            qe, ke, kf, bl = _gla_operands_bounded(q, k, gk)
        else:
            qs, ks, qe, ke, bl = _gla_operands(q, k, gk, GLA_CHUNK, BF16)
            masks = _level_masks(6)
        qk_sc[...] = _dot(hn, w_qk_ref[...])
        gk_new = _gk_from(_dot(hn, w_gd_ref[...]), wgate_ref, bgate_ref)
        gk_sc[...] = gk_new
        bounded_sc[0] = (_chunk_decay_floor(gk_new) >= -GLA_FAST_LIMIT).astype(jnp.int32)

        first_row = (lax.broadcasted_iota(jnp.int32, a.shape, 0) == 0) & seq_start
        mult_r = jnp.where(first_row, 1.0, mult)
        a8, h8 = _scan_affine8(a, mult_r * gi * xc)
        hs = _lru_carry(a8, h8, h_sc[SUBLANES - 1:SUBLANES, :])
        h_sc[...] = hs[-1]
        hlast_ref[...] = hs[-1]
        lru_out = _rms(jnp.concatenate(hs, axis=0) * jax.nn.gelu(yl), vec_ref[4:5, :])

        vb = v_sc[...]
        ps, vhs, upds = [], [], []
        for h in range(GLA_HEADS):
            grp, half = divmod(h, 2)
            lanes = slice(LANES * grp, LANES * (grp + 1))
            ps.append(_scores_bounded(qe, kf, h) if bounded else _scores(qs, ks, masks, h))
            vhs.append(vb[:, GLA_DV * h:GLA_DV * (h + 1)].reshape(nchunk, BLK, GLA_DV))
            upds.append(_bdot('csd,csv->cdv', _blocks(ke[half], lanes), vhs[h]))
        v_sc[...] = _dot(hn, w_v_ref[...]).astype(BF16)
        g_new = _dot(hn, w_g_ref[...])

        decs = [[_decay_rows(bl[c * BLK:c * BLK + 1, LANES * grp:LANES * (grp + 1)])
                 for c in range(nchunk)] for grp in range(GLA_HEADS // 2)]
        outs = []
        for h in range(GLA_HEADS):
            grp, half = divmod(h, 2)
            lanes = slice(LANES * grp, LANES * (grp + 1))
            st = s_sc[h]
            states = []
            for c in range(nchunk):
                states.append(st.astype(BF16))
                st = decs[grp][c] * st + upds[h][c]
            s_sc[h] = st
            sfin_ref[h] = st[GLA_DK * half:GLA_DK * (half + 1), :]
            o = (_bdot('cts,csv->ctv', ps[h], vhs[h])
                 + _bdot('ctd,cdv->ctv', _blocks(qe[half], lanes), jnp.stack(states)))
            outs.append(o.reshape(tt, GLA_DV))
        gla_out = _gla_finish(jnp.concatenate(outs, axis=1), g_sc[...], gn_ref[...])
        g_sc[...] = g_new

        o_ref[...] = _out_proj(x_sc[...], lru_out, gla_out, w_out_ref, gpost_ref)
        x_sc[...] = x_new

    lax.cond(bounded_sc[0] != 0, functools.partial(body, True), functools.partial(body, False))


def _weight_specs(l, const):
    d = D_MODEL
    return [
        pl.BlockSpec((None, None, 1, d), const(l, 1, 0, 0)),
        pl.BlockSpec((None, None, 1, d), const(l, 1, 0, 0)),
        pl.BlockSpec((None, d, 2 * D_LRU), const(l, 0, 0)),
        pl.BlockSpec((None, d, 2 * D_QK), const(l, 0, 0)),
        pl.BlockSpec((None, d, D_GLA), const(l, 0, 0)),
        pl.BlockSpec((None, d, D_GLA), const(l, 0, 0)),
        pl.BlockSpec((None, d, GLA_RANK), const(l, 0, 0)),
        pl.BlockSpec((None, CONV_W, D_LRU), const(l, 0, 0)),
        pl.BlockSpec((None, 5, D_LRU), const(l, 0, 0)),
        pl.BlockSpec((None, 2, MXU_DIM, MXU_DIM), const(l, 0, 0, 0)),
        pl.BlockSpec((None, 2, MXU_DIM, MXU_DIM), const(l, 0, 0, 0)),
        pl.BlockSpec((None, GLA_RANK, D_QK), const(l, 0, 0)),
        pl.BlockSpec((None, 1, D_QK), const(l, 0, 0)),
        pl.BlockSpec((None, 1, GLA_DV), const(l, 0, 0)),
        pl.BlockSpec((None, d, d), const(l, 0, 0)),
    ]


def _prompt_mixer_call(x_all, weights, l):
    tt = PROMPT_TT
    nt = SEQ // tt
    ntiles = BATCH * nt
    const = lambda *ix: (lambda s: ix)
    done = lambda s: jnp.maximum(s - 1, 0)
    out_shapes = (
        jax.ShapeDtypeStruct((N_ROWS, D_MODEL), F32),
        jax.ShapeDtypeStruct((BATCH, SUBLANES, D_LRU), F32),
        jax.ShapeDtypeStruct((BATCH, SUBLANES, D_LRU), F32),
        jax.ShapeDtypeStruct((BATCH, GLA_HEADS, GLA_DK, GLA_DV), F32),
    )
    out_specs = (
        pl.BlockSpec((tt, D_MODEL), lambda s: (done(s), 0)),
        pl.BlockSpec((None, SUBLANES, D_LRU), lambda s: (done(s) // nt, 0, 0)),
        pl.BlockSpec((None, SUBLANES, D_LRU), lambda s: (done(s) // nt, 0, 0)),
        pl.BlockSpec((None, GLA_HEADS, GLA_DK, GLA_DV), lambda s: (done(s) // nt, 0, 0, 0)),
    )
    scratch = [
        pltpu.VMEM((SUBLANES, D_LRU), F32),
        pltpu.VMEM((SUBLANES, D_LRU), F32),
        pltpu.VMEM((GLA_HEADS, LANES, GLA_DV), F32),
        pltpu.VMEM((tt, D_MODEL), F32),
        pltpu.VMEM((tt, 2 * D_LRU), F32),
        pltpu.VMEM((tt, 2 * D_QK), F32),
        pltpu.VMEM((tt, D_QK), F32),
        pltpu.VMEM((tt, D_GLA), BF16),
        pltpu.VMEM((tt, D_GLA), F32),
        pltpu.SMEM((1,), jnp.int32),
    ]
    return pl.pallas_call(
        _prompt_mixer_kernel,
        grid=(ntiles + 1,),
        in_specs=[pl.BlockSpec((tt, D_MODEL), lambda s: (jnp.minimum(s, ntiles - 1), 0))]
                 + _weight_specs(l, const),
        out_specs=out_specs,
        out_shape=out_shapes,
        scratch_shapes=scratch,
        input_output_aliases={0: 0},
        compiler_params=pltpu.CompilerParams(
            dimension_semantics=("arbitrary",), vmem_limit_bytes=VMEM_LIMIT),
        name="mixer_prompt",
    )(x_all, *weights)


def _sample_mixer_kernel(
        x_ref, gpre_ref, gpost_ref, w_lru_ref, w_qk_ref, w_v_ref, w_g_ref, w_gd_ref,
        convw_ref, vec_ref, wa_ref, wx_ref, wgate_ref, bgate_ref, gn_ref, w_out_ref,
        h0_ref, tail_ref, s0_ref, *rest, carried):
    o_ref, hseq_ref, xl_ref, snew_ref = rest[1:] if carried else rest
    rows = SAMPLE_BT * DEC_SEQ
    nseq = SAMPLE_BT
    x = x_ref[...]
    hn = _rms(x, gpre_ref[...]).astype(BF16)
    xy = _dot(hn, w_lru_ref[...])
    xl, yl = xy[:, :D_LRU], xy[:, D_LRU:]
    qk = _dot(hn, w_qk_ref[...])
    q, k = qk[:, :D_QK] * (GLA_DK ** -0.5), qk[:, D_QK:]
    v = _dot(hn, w_v_ref[...])
    g = _dot(hn, w_g_ref[...])
    gd = _dot(hn, w_gd_ref[...])

    t8 = _row_in_seg(xl.shape, DEC_SEQ)
    tail = tail_ref[...]
    xc = _causal_conv(
        xl, convw_ref, vec_ref[0:1, :],
        lambda kk: jnp.where(t8 < kk, _shift_down(tail, kk, in_group=True),
                             _shift_down(xl, kk, in_group=True)))
    xl_ref[...] = xl

    a, mult, gi = _lru_gates(xc, vec_ref, wa_ref, wx_ref)
    bterm = mult * gi * xc + a * h0_ref[...]
    _, hseq = _scan_affine8(a, bterm)
    hseq_ref[...] = hseq
    lru_out = _rms(hseq * jax.nn.gelu(yl), vec_ref[4:5, :])

    gk = _gk_from(gd, wgate_ref, bgate_ref)
    qs, ks, qe, ke, bl = _gla_operands(q, k, gk, DEC_SEQ, F32)
    masks = _level_masks(3)
    vb = v.astype(BF16)
    zeros_half = jnp.zeros((nseq, GLA_DK, GLA_DV), F32)

    def per_seq(x2):
        return x2.reshape(nseq, DEC_SEQ, x2.shape[1]).astype(BF16)

    decs = []
    for grp in range(GLA_HEADS // 2):
        bl3 = bl[:, LANES * grp:LANES * (grp + 1)].reshape(nseq, DEC_SEQ, LANES)
        decs.append([_decay_rows(bl3[b, 0:1, :]) for b in range(nseq)])
    outs = []
    for h in range(GLA_HEADS):
        grp, half = divmod(h, 2)
        lanes = slice(LANES * grp, LANES * (grp + 1))
        cols = slice(GLA_DV * h, GLA_DV * (h + 1))
        ks_rows = slice(GLA_DK * half, GLA_DK * (half + 1))
        p = _scores(qs, ks, masks, h)
        vh = vb[:, cols].reshape(rows // BLK, BLK, GLA_DV)
        o_intra = _bdot('cts,csv->ctv', p, vh).reshape(rows, GLA_DV)
        s0 = s0_ref[:, h]
        s_pad = jnp.concatenate([zeros_half, s0] if half else [s0, zeros_half], axis=1)
        o_inter = _bdot('btd,bdv->btv', per_seq(qe[half][:, lanes]), s_pad.astype(BF16))
        outs.append(o_intra + o_inter.reshape(rows, GLA_DV))
        upd = _bdot('bsd,bsv->bdv', per_seq(ke[half][:, lanes]), per_seq(v[:, cols]))
        for b in range(nseq):
            snew_ref[b, h] = decs[grp][b][ks_rows] * s0[b] + upd[b, ks_rows, :]
    gla_out = _gla_finish(jnp.concatenate(outs, axis=1), g, gn_ref[...])

    o_ref[...] = _out_proj(x, lru_out, gla_out, w_out_ref, gpost_ref)


def _sample_mixer_call(x_all, weights, h0pad, tailpad, s0, snew_all, l):
    rows = SAMPLE_BT * DEC_SEQ
    nb = DEC_BATCH // SAMPLE_BT
    base = N_PROMPT // rows
    const = lambda *ix: (lambda i: ix)
    st_spec = pl.BlockSpec((None, SAMPLE_BT, GLA_HEADS, GLA_DK, GLA_DV),
                           lambda i: (l, i, 0, 0, 0))
    out_shapes = (
        jax.ShapeDtypeStruct((N_ROWS, D_MODEL), F32),
        jax.ShapeDtypeStruct((N_SAMPLE, D_LRU), F32),
        jax.ShapeDtypeStruct((N_SAMPLE, D_LRU), F32),
        jax.ShapeDtypeStruct((DEPTH, DEC_BATCH, GLA_HEADS, GLA_DK, GLA_DV), F32),
    )
    out_specs = (
        pl.BlockSpec((rows, D_MODEL), lambda i: (base + i, 0)),
        pl.BlockSpec((rows, D_LRU), lambda i: (i, 0)),
        pl.BlockSpec((rows, D_LRU), lambda i: (i, 0)),
        st_spec,
    )
    in_specs = ([pl.BlockSpec((rows, D_MODEL), lambda i: (base + i, 0))]
                + _weight_specs(l, const)
                + [pl.BlockSpec((None, rows, D_LRU), lambda i: (l, i, 0)),
                   pl.BlockSpec((None, rows, D_LRU), lambda i: (l, i, 0)),
                   st_spec])
    args = [x_all, *weights, h0pad, tailpad, s0]
    aliases = {0: 0}
    if snew_all is not None:
        aliases[len(args)] = 3
        in_specs.append(pl.BlockSpec(memory_space=pl.ANY))
        args.append(snew_all)
    return pl.pallas_call(
        functools.partial(_sample_mixer_kernel, carried=snew_all is not None),
        grid=(nb,),
        in_specs=in_specs,
        out_specs=out_specs,
        out_shape=out_shapes,
        input_output_aliases=aliases,
        compiler_params=pltpu.CompilerParams(
            dimension_semantics=("parallel",), vmem_limit_bytes=VMEM_LIMIT),
        name="mixer_sample",
    )(*args)


def _blockdiag(w):
    per = MXU_DIM // LRU_BLOCK
    w = w.reshape(DEPTH, N_LRU_BLOCKS // per, per, LRU_BLOCK, LRU_BLOCK)
    eye = jnp.eye(per, dtype=w.dtype)
    bd = jnp.einsum('lpiab,ij->lpiajb', w, eye)
    return bd.reshape(DEPTH, N_LRU_BLOCKS // per, MXU_DIM, MXU_DIM)


def kernel(x_prompt, x_sample, state_lru_h, state_lru_conv, state_gla, norm_pre, norm_post, w_ffn_gate, w_ffn_up, w_ffn_down, w_in, conv_w, conv_b, lru_wa, lru_ba, lru_wx, lru_bx, lru_lambda, lru_norm, gla_w_gate, gla_b_gate, gla_norm, w_out):
    gpre4 = norm_pre.reshape(DEPTH, 3, 1, D_MODEL)
    gpost4 = norm_post.reshape(DEPTH, 3, 1, D_MODEL)
    w_ffn_f32 = (w_ffn_gate, w_ffn_up, w_ffn_down)
    c1 = 2 * D_LRU
    c2 = c1 + 2 * D_QK
    c3 = c2 + D_GLA
    c4 = c3 + D_GLA
    vec = jnp.stack([conv_b, lru_ba, lru_bx, lru_lambda, lru_norm], axis=1)
    weights = (
        gpre4, gpost4,
        w_in[:, :, :c1].astype(BF16), w_in[:, :, c1:c2].astype(BF16),
        w_in[:, :, c2:c3].astype(BF16), w_in[:, :, c3:c4].astype(BF16),
        w_in[:, :, c4:].astype(BF16),
        conv_w, vec,
        _blockdiag(lru_wa).astype(BF16), _blockdiag(lru_wx).astype(BF16),
        gla_w_gate.astype(BF16), gla_b_gate.reshape(DEPTH, 1, D_QK),
        gla_norm.reshape(DEPTH, 1, GLA_DV),
        w_out.astype(BF16),
    )
    h0pad = jnp.pad(state_lru_h[:, :, None, :],
                    ((0, 0), (0, 0), (0, DEC_SEQ - 1), (0, 0))).reshape(DEPTH, N_SAMPLE, D_LRU)
    tailpad = jnp.pad(state_lru_conv,
                      ((0, 0), (0, 0), (SUBLANES - (CONV_W - 1), 0), (0, 0))
                      ).reshape(DEPTH, N_SAMPLE, D_LRU)

    w_bf16 = tuple(w[0, 0].astype(BF16) for w in w_ffn_f32)
    ffn_order = [(l, idx) for l in range(DEPTH) for idx in range(2)]

    def ffn(xs, l, idx, first=False, last=False):
        pos = ffn_order.index((l, idx))
        nxt = ffn_order[pos + 1] if pos + 1 < len(ffn_order) else None
        return _ffn_call(xs, gpre4, gpost4, w_bf16, w_ffn_f32 if nxt else None, l, idx, nxt,
                         "ab"[idx], first, last)

    hp, cp, sp, hs, cs = [], [], [], [], []
    snew_all = None
    xs = (x_prompt.reshape(N_PROMPT, D_MODEL), x_sample.reshape(N_SAMPLE, D_MODEL))
    for l in range(DEPTH):
        (x_all,), w_bf16 = ffn(xs, l, 0, first=(l == 0))
        x_all, hl, ct, sf = _prompt_mixer_call(x_all, weights, l)
        hp.append(hl[:, SUBLANES - 1])
        cp.append(ct[:, SUBLANES - (CONV_W - 1):])
        sp.append(sf)
        x_all, hseq, xl, snew_all = _sample_mixer_call(
            x_all, weights, h0pad, tailpad, state_gla, snew_all, l)
        hs.append(hseq.reshape(DEC_BATCH, DEC_SEQ, D_LRU)[:, DEC_SEQ - 1])
        cs.append(xl.reshape(DEC_BATCH, DEC_SEQ, D_LRU)[:, DEC_SEQ - (CONV_W - 1):])
        xs, w_bf16 = ffn((x_all,), l, 1, last=(l == DEPTH - 1))

    yp = xs[0].reshape(BATCH, SEQ, D_MODEL)
    ys = xs[1].reshape(DEC_BATCH, DEC_SEQ, D_MODEL)
    return (yp, ys, jnp.stack(hp), jnp.stack(cp), jnp.stack(sp),
            jnp.stack(hs), jnp.stack(cs), snew_all)
```

```python
import functools

import jax
import jax.numpy as jnp
from jax import lax
from jax.experimental import pallas as pl
from jax.experimental.pallas import tpu as pltpu

D_MODEL = 1024
BATCH = 8
SEQ = 2048
DEPTH = 4
DEC_BATCH = 128
DEC_SEQ = 8
D_LRU = 512
N_LRU_BLOCKS = 8
LRU_BLOCK = 64
CONV_W = 4
LRU_C = 8.0
D_GLA = 512
GLA_HEADS = 4
GLA_DV = 128
GLA_DK = 64
GLA_RANK = 16
GLA_NORMALIZER = 16.0
GLA_CHUNK = 64
D_FF = 2816
EPS = 1e-6
D_QK = GLA_HEADS * GLA_DK
D_IN = 2 * D_LRU + 2 * D_QK + 2 * D_GLA + GLA_RANK

N_PROMPT = BATCH * SEQ
N_SAMPLE = DEC_BATCH * DEC_SEQ
N_ROWS = N_PROMPT + N_SAMPLE

SUBLANES = 8
LANES = 128
MXU_DIM = 256
VMEM_LIMIT = 56 * 1024 * 1024

FFN_TM = 1024
FFN_TM_FIRST = 512
FFN_FC = 256
PROMPT_TT = 512
SAMPLE_BT = 32
BLK = 64
GLA_FAST_LIMIT = 40.0

F32 = jnp.float32
BF16 = jnp.bfloat16


def _rms(x, g):
    ms = jnp.mean(x * x, axis=-1, keepdims=True)
    return x * lax.rsqrt(ms + EPS) * g


def _softplus(x):
    return jnp.maximum(x, 0.0) + jnp.log(1.0 + jnp.exp(-jnp.abs(x)))


def _dot(a, b):
    return jnp.dot(a, b, preferred_element_type=F32)


def _bdot(spec, a, b):
    return jnp.einsum(spec, a, b, preferred_element_type=F32)


N_FC = D_FF // FFN_FC
MIX_ROWS = 64
N_MIX_CHUNKS = D_MODEL // MIX_ROWS


def _ffn_kernel(*refs, first, last, convert, mix, np_tiles):
    it = iter(refs)
    x_refs = (next(it), next(it)) if first else (next(it),)
    gpre_ref, gpost_ref, wg_ref, wu_ref, wd_ref = (next(it) for _ in range(5))
    nxt = tuple(next(it) for _ in range(3)) if convert else ()
    mix_in = tuple(next(it) for _ in range(2)) if mix else ()
    o_refs = (next(it), next(it)) if last else (next(it),)
    cvt = tuple(next(it) for _ in range(3)) if convert else ()
    mix_out = tuple(next(it) for _ in range(2)) if mix else ()
    i = pl.program_id(0)

    if first:
        x_sc = next(it)

        @pl.when(i < np_tiles)
        def _():
            x_sc[...] = x_refs[0][...]

        @pl.when(i >= np_tiles)
        def _():
            x_sc[...] = x_refs[1][...]

        x = x_sc[...]
    else:
        x = x_refs[0][...]

    h = _rms(x, gpre_ref[...]).astype(BF16)
    acc = jnp.zeros((x.shape[0], D_MODEL), F32)
    for c0 in range(0, D_FF, FFN_FC):
        g = _dot(h, wg_ref[:, c0:c0 + FFN_FC])
        u = _dot(h, wu_ref[:, c0:c0 + FFN_FC])
        act = (jax.nn.silu(g) * u).astype(BF16)
        acc = acc + _dot(act, wd_ref[c0:c0 + FFN_FC, :])
    out = x + 0.5 * _rms(acc, gpost_ref[...])

    if last:
        @pl.when(i < np_tiles)
        def _():
            o_refs[0][...] = out

        @pl.when(i >= np_tiles)
        def _():
            o_refs[1][...] = out
    else:
        o_refs[0][...] = out

    if convert:
        @pl.when(i < N_FC)
        def _():
            for src, dst in zip(nxt, cvt):
                dst[...] = src[...].astype(BF16)

    if mix:
        @pl.when(i < N_MIX_CHUNKS)
        def _():
            for src, dst in zip(mix_in, mix_out):
                dst[...] = src[...].astype(BF16)


def _ffn_call(xs, gpre4, gpost4, w_bf16, w_f32_next, mix_f32, l, idx, nxt_l_idx, sub, first,
              last):
    convert = w_f32_next is not None
    mix = mix_f32 is not None
    const = lambda *ix: (lambda i: ix)
    nidx = 2 * idx
    tm = FFN_TM_FIRST if first else FFN_TM
    np_tiles = N_PROMPT // tm
    prompt_ix = lambda i: (jnp.minimum(i, np_tiles - 1), 0)
    sample_ix = lambda i: (jnp.maximum(i - np_tiles, 0), 0)
    row_spec = lambda ix: pl.BlockSpec((tm, D_MODEL), ix)
    chunk = lambda i: jnp.minimum(i, N_FC - 1)

    in_specs = [row_spec(prompt_ix), row_spec(sample_ix)] if first else [row_spec(lambda i: (i, 0))]
    in_specs += [
        pl.BlockSpec((None, None, 1, D_MODEL), const(l, nidx, 0, 0)),
        pl.BlockSpec((None, None, 1, D_MODEL), const(l, nidx, 0, 0)),
        pl.BlockSpec((D_MODEL, D_FF), const(0, 0), pipeline_mode=pl.Buffered(1)),
        pl.BlockSpec((D_MODEL, D_FF), const(0, 0), pipeline_mode=pl.Buffered(1)),
        pl.BlockSpec((D_FF, D_MODEL), const(0, 0), pipeline_mode=pl.Buffered(1)),
    ]
    args = list(xs) + [gpre4, gpost4] + list(w_bf16)
    if convert:
        nl, ni = nxt_l_idx
        in_specs += [
            pl.BlockSpec((None, None, D_MODEL, FFN_FC), lambda i: (nl, ni, 0, chunk(i))),
            pl.BlockSpec((None, None, D_MODEL, FFN_FC), lambda i: (nl, ni, 0, chunk(i))),
            pl.BlockSpec((None, None, FFN_FC, D_MODEL), lambda i: (nl, ni, chunk(i), 0)),
        ]
        args += list(w_f32_next)
    mchunk = lambda i: jnp.minimum(i, N_MIX_CHUNKS - 1)
    if mix:
        in_specs += [
            pl.BlockSpec((None, MIX_ROWS, D_IN), lambda i: (l, mchunk(i), 0)),
            pl.BlockSpec((None, MIX_ROWS, D_MODEL), lambda i: (l, mchunk(i), 0)),
        ]
        args += list(mix_f32)

    if last:
        out_specs = [row_spec(prompt_ix), row_spec(sample_ix)]
        out_shape = [jax.ShapeDtypeStruct((N_PROMPT, D_MODEL), F32),
                     jax.ShapeDtypeStruct((N_SAMPLE, D_MODEL), F32)]
    else:
        out_specs = [row_spec(lambda i: (i, 0))]
        out_shape = [jax.ShapeDtypeStruct((N_ROWS, D_MODEL), F32)]
    n_act = len(out_specs)
    if convert:
        out_specs += [
            pl.BlockSpec((D_MODEL, FFN_FC), lambda i: (0, chunk(i))),
            pl.BlockSpec((D_MODEL, FFN_FC), lambda i: (0, chunk(i))),
            pl.BlockSpec((FFN_FC, D_MODEL), lambda i: (chunk(i), 0)),
        ]
        out_shape += [jax.ShapeDtypeStruct((D_MODEL, D_FF), BF16),
                      jax.ShapeDtypeStruct((D_MODEL, D_FF), BF16),
                      jax.ShapeDtypeStruct((D_FF, D_MODEL), BF16)]
    n_cvt = len(out_specs)
    if mix:
        out_specs += [
            pl.BlockSpec((MIX_ROWS, D_IN), lambda i: (mchunk(i), 0)),
            pl.BlockSpec((MIX_ROWS, D_MODEL), lambda i: (mchunk(i), 0)),
        ]
        out_shape += [jax.ShapeDtypeStruct((D_MODEL, D_IN), BF16),
                      jax.ShapeDtypeStruct((D_MODEL, D_MODEL), BF16)]

    res = pl.pallas_call(
        functools.partial(_ffn_kernel, first=first, last=last, convert=convert, mix=mix,
                          np_tiles=np_tiles),
        grid=(N_ROWS // tm,),
        in_specs=in_specs,
        out_specs=out_specs,
        out_shape=out_shape,
        scratch_shapes=[pltpu.VMEM((tm, D_MODEL), F32)] if first else [],
        input_output_aliases={} if (first or last) else {0: 0},
        compiler_params=pltpu.CompilerParams(
            dimension_semantics=("arbitrary",), vmem_limit_bytes=VMEM_LIMIT),
        name=f"ffn_{sub}",
    )(*args)
    return res[:n_act], tuple(res[n_act:n_cvt]), tuple(res[n_cvt:])


def _grouped(x):
    return x.reshape(x.shape[0] // SUBLANES, SUBLANES, x.shape[1])


def _sub_iota(x3):
    return lax.broadcasted_iota(jnp.int32, x3.shape, 1)


def _groups_down(x3, q, first=None):
    if q == 0:
        return x3
    head = jnp.zeros((q,) + x3.shape[1:], x3.dtype) if first is None else first
    return jnp.concatenate([head, x3[:x3.shape[0] - q]], axis=0)


def _groups_up(x3, q):
    if q == 0:
        return x3
    return jnp.concatenate([x3[q:], jnp.zeros((q,) + x3.shape[1:], x3.dtype)], axis=0)


def _shift_down(x, k, first=None, in_group=False):
    x3 = _grouped(x)
    q, r = divmod(k, SUBLANES)
    if r == 0:
        y = _groups_down(x3, q)
    else:
        xr = pltpu.roll(x3, r, 1)
        if in_group:
            y = xr
        else:
            fr = None if first is None else pltpu.roll(_grouped(first), r, 1)
            y = jnp.where(_sub_iota(x3) < r, _groups_down(xr, q + 1, fr if q == 0 else None),
                          _groups_down(xr, q))
    return y.reshape(x.shape)


def _shift_up(x, k, in_group=False):
    x3 = _grouped(x)
    q, r = divmod(k, SUBLANES)
    if r == 0:
        y = _groups_up(x3, q)
    else:
        xr = pltpu.roll(x3, SUBLANES - r, 1)
        if in_group:
            y = xr
        else:
            y = jnp.where(_sub_iota(x3) >= SUBLANES - r, _groups_up(xr, q + 1), _groups_up(xr, q))
    return y.reshape(x.shape)


def _row_in_seg(shape, seg):
    return lax.broadcasted_iota(jnp.int32, shape, 0) & (seg - 1)


def _scan_affine8(a, b):
    t = _row_in_seg(a.shape, SUBLANES)
    d = 1
    while d < SUBLANES:
        m = t >= d
        b = jnp.where(m, a * _shift_down(b, d, in_group=True) + b, b)
        a = jnp.where(m, a * _shift_down(a, d, in_group=True), a)
        d *= 2
    return a, b


def _seg_cumsum(x, seg):
    t = _row_in_seg(x.shape, seg)
    d = 1
    while d < seg:
        x = jnp.where(t >= d, x + _shift_down(x, d, in_group=(seg == SUBLANES)), x)
        d *= 2
    return x


def _causal_conv(xl, convw_ref, bias, shifted):
    xc = None
    for jj in range(CONV_W):
        kk = CONV_W - 1 - jj
        term = (xl if kk == 0 else shifted(kk)) * convw_ref[jj:jj + 1, :]
        xc = term if xc is None else xc + term
    return xc + bias


def _lru_gates(xc, vec_ref, wa_ref, wx_ref):
    half = D_LRU // 2
    xb = xc.astype(BF16)
    xb0, xb1 = xb[:, :half], xb[:, half:]
    ba, bx, lam = vec_ref[1:2, :], vec_ref[2:3, :], vec_ref[3:4, :]
    r = jax.nn.sigmoid(
        jnp.concatenate([_dot(xb0, wa_ref[0]), _dot(xb1, wa_ref[1])], axis=1) + ba)
    i = jax.nn.sigmoid(
        jnp.concatenate([_dot(xb0, wx_ref[0]), _dot(xb1, wx_ref[1])], axis=1) + bx)
    log_a = -LRU_C * r * _softplus(-lam)
    a = jnp.exp(log_a)
    mult = jnp.sqrt(-jnp.tanh(log_a) * (a * a + 1.0))
    return a, mult, i


def _lru_carry(a8, h8, hin):
    hs = []
    for gidx in range(a8.shape[0] // SUBLANES):
        rows = slice(SUBLANES * gidx, SUBLANES * (gidx + 1))
        hg = a8[rows] * hin + h8[rows]
        hin = hg[SUBLANES - 1:SUBLANES, :]
        hs.append(hg)
    return hs


def _level_masks(nlev):
    t = lax.broadcasted_iota(jnp.int32, (1, BLK, BLK), 1)
    s = lax.broadcasted_iota(jnp.int32, (1, BLK, BLK), 2)
    masks = [t == s]
    for lv in range(nlev):
        masks.append((((t >> lv) ^ (s >> lv)) == 1) & (((t >> lv) & 1) == 1))
    return masks


def _head_halves(val, dtype):
    second = (lax.broadcasted_iota(jnp.int32, val.shape, 1) & (LANES - 1)) >= GLA_DK
    zero = jnp.zeros_like(val)
    return (jnp.where(second, zero, val).astype(dtype), jnp.where(second, val, zero).astype(dtype))


def _gla_operands(q, k, gk, seg, state_dtype):
    t = lax.broadcasted_iota(jnp.int32, gk.shape, 0)
    b = _seg_cumsum(gk, seg)
    qs = [_head_halves(q, BF16)]
    ks = [k.astype(BF16)]
    ref = b - gk
    end = b
    s = 1
    while s < seg:
        qs.append(_head_halves(q * jnp.exp(b - ref), BF16))
        ks.append((k * jnp.exp(end - b)).astype(BF16))
        bit = (t & s) != 0
        ref = jnp.where(bit, _shift_down(ref, s, in_group=True), ref)
        end = jnp.where(bit, end, _shift_up(end, s, in_group=True))
        s *= 2
    qe = _head_halves(q * jnp.exp(b), state_dtype)
    ke = _head_halves(k * jnp.exp(end - b), state_dtype)
    return qs, ks, qe, ke, end


def _gla_operands_bounded(q, k, gk):
    n, c = gk.shape
    b = _seg_cumsum(gk, GLA_CHUNK)
    b3 = b.reshape(n // GLA_CHUNK, GLA_CHUNK, c)
    total = jnp.broadcast_to(b3[:, GLA_CHUNK - 1:GLA_CHUNK, :], b3.shape).reshape(n, c)
    qe = _head_halves(q * jnp.exp(b), BF16)
    ke = _head_halves(k * jnp.exp(total - b), BF16)
    kf = (k * jnp.exp(-b)).astype(BF16)
    return qe, ke, kf, total


def _chunk_decay_floor(gk):
    n, c = gk.shape
    return jnp.min(jnp.sum(gk.reshape(n // GLA_CHUNK, GLA_CHUNK, c), axis=1))


def _blocks(x, lanes):
    return x[:, lanes].reshape(x.shape[0] // BLK, BLK, LANES)


def _scores_bounded(qe, kf, h):
    grp, half = divmod(h, 2)
    lanes = slice(LANES * grp, LANES * (grp + 1))
    t = lax.broadcasted_iota(jnp.int32, (1, BLK, BLK), 1)
    s = lax.broadcasted_iota(jnp.int32, (1, BLK, BLK), 2)
    sc = _bdot('ctd,csd->cts', _blocks(qe[half], lanes), _blocks(kf, lanes))
    return jnp.where(t >= s, sc, 0.0).astype(BF16)


def _scores(qs, ks, masks, h):
    grp, half = divmod(h, 2)
    lanes = slice(LANES * grp, LANES * (grp + 1))
    p = None
    for lv, m in enumerate(masks):
        sc = _bdot('ctd,csd->cts', _blocks(qs[lv][half], lanes), _blocks(ks[lv], lanes))
        p = jnp.where(m, sc, 0.0 if p is None else p)
    return p.astype(BF16)


def _decay_rows(bl_row):
    return jnp.exp(jnp.transpose(jnp.broadcast_to(bl_row, (LANES, LANES))))


def _gla_finish(o, g, gn):
    outs = []
    for h in range(GLA_HEADS):
        sl = slice(GLA_DV * h, GLA_DV * (h + 1))
        outs.append(_rms(o[:, sl], gn) * jax.nn.silu(g[:, sl]))
    return jnp.concatenate(outs, axis=1)


def _gk_from(gd, wgate_ref, bgate_ref):
    z = _dot(gd.astype(BF16), wgate_ref[...]) + bgate_ref[...]
    return -_softplus(-z) * (1.0 / GLA_NORMALIZER)


def _out_proj(x, lru_out, gla_out, w_out_ref, gpost_ref):
    m = (_dot(lru_out.astype(BF16), w_out_ref[:D_LRU, :])
         + _dot(gla_out.astype(BF16), w_out_ref[D_LRU:, :]))
    return x + _rms(m, gpost_ref[...])


def _prompt_mixer_kernel(
        x_ref, gpre_ref, gpost_ref, w_lru_ref, w_qk_ref, w_v_ref, w_g_ref, w_gd_ref,
        convw_ref, vec_ref, wa_ref, wx_ref, wgate_ref, bgate_ref, gn_ref, w_out_ref,
        o_ref, hlast_ref, ctail_ref, sfin_ref,
        tail_sc, h_sc, s_sc, x_sc, xy_sc, qk_sc, gk_sc, v_sc, g_sc, bounded_sc):
    s = pl.program_id(0)
    tt = PROMPT_TT
    nt = SEQ // tt
    nchunk = tt // BLK
    seq_start = lax.rem(s + nt - 1, nt) == 0

    @pl.when(s == 0)
    def _():
        for ref in (x_sc, xy_sc, qk_sc, gk_sc, v_sc, g_sc):
            ref[...] = jnp.zeros_like(ref)
        bounded_sc[0] = 1

    @pl.when((s == 0) | seq_start)
    def _():
        tail_sc[...] = jnp.zeros_like(tail_sc)
        h_sc[...] = jnp.zeros_like(h_sc)
        s_sc[...] = jnp.zeros_like(s_sc)

    def body(bounded):
        x_new = x_ref[...]
        hn = _rms(x_new, gpre_ref[...]).astype(BF16)

        xy = xy_sc[...]
        xl, yl = xy[:, :D_LRU], xy[:, D_LRU:]
        tail = tail_sc[...]
        xc = _causal_conv(xl, convw_ref, vec_ref[0:1, :],
                          lambda kk: _shift_down(xl, kk, first=tail))
        tail_sc[...] = xl[tt - SUBLANES:, :]
        ctail_ref[...] = xl[tt - SUBLANES:, :]
        xy_sc[...] = _dot(hn, w_lru_ref[...])
        a, mult, gi = _lru_gates(xc, vec_ref, wa_ref, wx_ref)

        qk = qk_sc[...]
        q, k = qk[:, :D_QK] * (GLA_DK ** -0.5), qk[:, D_QK:]
        gk = gk_sc[...]
        if bounded:
            qe, ke, kf, bl = _gla_operands_bounded(q, k, gk)
        else:
            qs, ks, qe, ke, bl = _gla_operands(q, k, gk, GLA_CHUNK, BF16)
            masks = _level_masks(6)
        qk_sc[...] = _dot(hn, w_qk_ref[...])
        gk_new = _gk_from(_dot(hn, w_gd_ref[...]), wgate_ref, bgate_ref)
        gk_sc[...] = gk_new
        bounded_sc[0] = (_chunk_decay_floor(gk_new) >= -GLA_FAST_LIMIT).astype(jnp.int32)

        first_row = (lax.broadcasted_iota(jnp.int32, a.shape, 0) == 0) & seq_start
        mult_r = jnp.where(first_row, 1.0, mult)
        a8, h8 = _scan_affine8(a, mult_r * gi * xc)
        hs = _lru_carry(a8, h8, h_sc[SUBLANES - 1:SUBLANES, :])
        h_sc[...] = hs[-1]
        hlast_ref[...] = hs[-1]
        lru_out = _rms(jnp.concatenate(hs, axis=0) * jax.nn.gelu(yl), vec_ref[4:5, :])

        vb = v_sc[...]
        ps, vhs, upds = [], [], []
        for h in range(GLA_HEADS):
            grp, half = divmod(h, 2)
            lanes = slice(LANES * grp, LANES * (grp + 1))
            ps.append(_scores_bounded(qe, kf, h) if bounded else _scores(qs, ks, masks, h))
            vhs.append(vb[:, GLA_DV * h:GLA_DV * (h + 1)].reshape(nchunk, BLK, GLA_DV))
            upds.append(_bdot('csd,csv->cdv', _blocks(ke[half], lanes), vhs[h]))
        v_sc[...] = _dot(hn, w_v_ref[...]).astype(BF16)
        g_new = _dot(hn, w_g_ref[...])

        decs = [[_decay_rows(bl[c * BLK:c * BLK + 1, LANES * grp:LANES * (grp + 1)])
                 for c in range(nchunk)] for grp in range(GLA_HEADS // 2)]
        outs = []
        for h in range(GLA_HEADS):
            grp, half = divmod(h, 2)
            lanes = slice(LANES * grp, LANES * (grp + 1))
            st = s_sc[h]
            states = []
            for c in range(nchunk):
                states.append(st.astype(BF16))
                st = decs[grp][c] * st + upds[h][c]
            s_sc[h] = st
            sfin_ref[h] = st[GLA_DK * half:GLA_DK * (half + 1), :]
            o = (_bdot('cts,csv->ctv', ps[h], vhs[h])
                 + _bdot('ctd,cdv->ctv', _blocks(qe[half], lanes), jnp.stack(states)))
            outs.append(o.reshape(tt, GLA_DV))
        gla_out = _gla_finish(jnp.concatenate(outs, axis=1), g_sc[...], gn_ref[...])
        g_sc[...] = g_new

        o_ref[...] = _out_proj(x_sc[...], lru_out, gla_out, w_out_ref, gpost_ref)
        x_sc[...] = x_new

    lax.cond(bounded_sc[0] != 0, functools.partial(body, True), functools.partial(body, False))


def _weight_specs(l, const):
    d = D_MODEL
    return [
        pl.BlockSpec((None, None, 1, d), const(l, 1, 0, 0)),
        pl.BlockSpec((None, None, 1, d), const(l, 1, 0, 0)),
        pl.BlockSpec((d, 2 * D_LRU), const(0, 0)),
        pl.BlockSpec((d, 2 * D_QK), const(0, 2)),
        pl.BlockSpec((d, D_GLA), const(0, 3)),
        pl.BlockSpec((d, D_GLA), const(0, 4)),
        pl.BlockSpec((None, d, GLA_RANK), const(l, 0, 0)),
        pl.BlockSpec((None, CONV_W, D_LRU), const(l, 0, 0)),
        pl.BlockSpec((None, 5, D_LRU), const(l, 0, 0)),
        pl.BlockSpec((None, 2, MXU_DIM, MXU_DIM), const(l, 0, 0, 0)),
        pl.BlockSpec((None, 2, MXU_DIM, MXU_DIM), const(l, 0, 0, 0)),
        pl.BlockSpec((None, GLA_RANK, D_QK), const(l, 0, 0)),
        pl.BlockSpec((None, 1, D_QK), const(l, 0, 0)),
        pl.BlockSpec((None, 1, GLA_DV), const(l, 0, 0)),
        pl.BlockSpec((d, d), const(0, 0)),
    ]


def _prompt_mixer_call(x_all, weights, l):
    tt = PROMPT_TT
    nt = SEQ // tt
    ntiles = BATCH * nt
    const = lambda *ix: (lambda s: ix)
    done = lambda s: jnp.maximum(s - 1, 0)
    out_shapes = (
        jax.ShapeDtypeStruct((N_ROWS, D_MODEL), F32),
        jax.ShapeDtypeStruct((BATCH, SUBLANES, D_LRU), F32),
        jax.ShapeDtypeStruct((BATCH, SUBLANES, D_LRU), F32),
        jax.ShapeDtypeStruct((BATCH, GLA_HEADS, GLA_DK, GLA_DV), F32),
    )
    out_specs = (
        pl.BlockSpec((tt, D_MODEL), lambda s: (done(s), 0)),
        pl.BlockSpec((None, SUBLANES, D_LRU), lambda s: (done(s) // nt, 0, 0)),
        pl.BlockSpec((None, SUBLANES, D_LRU), lambda s: (done(s) // nt, 0, 0)),
        pl.BlockSpec((None, GLA_HEADS, GLA_DK, GLA_DV), lambda s: (done(s) // nt, 0, 0, 0)),
    )
    scratch = [
        pltpu.VMEM((SUBLANES, D_LRU), F32),
        pltpu.VMEM((SUBLANES, D_LRU), F32),
        pltpu.VMEM((GLA_HEADS, LANES, GLA_DV), F32),
        pltpu.VMEM((tt, D_MODEL), F32),
        pltpu.VMEM((tt, 2 * D_LRU), F32),
        pltpu.VMEM((tt, 2 * D_QK), F32),
        pltpu.VMEM((tt, D_QK), F32),
        pltpu.VMEM((tt, D_GLA), BF16),
        pltpu.VMEM((tt, D_GLA), F32),
        pltpu.SMEM((1,), jnp.int32),
    ]
    return pl.pallas_call(
        _prompt_mixer_kernel,
        grid=(ntiles + 1,),
        in_specs=[pl.BlockSpec((tt, D_MODEL), lambda s: (jnp.minimum(s, ntiles - 1), 0))]
                 + _weight_specs(l, const),
        out_specs=out_specs,
        out_shape=out_shapes,
        scratch_shapes=scratch,
        input_output_aliases={0: 0},
        compiler_params=pltpu.CompilerParams(
            dimension_semantics=("arbitrary",), vmem_limit_bytes=VMEM_LIMIT),
        name="mixer_prompt",
    )(x_all, *weights)


def _sample_mixer_kernel(
        x_ref, gpre_ref, gpost_ref, w_lru_ref, w_qk_ref, w_v_ref, w_g_ref, w_gd_ref,
        convw_ref, vec_ref, wa_ref, wx_ref, wgate_ref, bgate_ref, gn_ref, w_out_ref,
        h0_ref, tail_ref, s0_ref, *rest, carried):
    o_ref, hseq_ref, xl_ref, snew_ref = rest[1:] if carried else rest
    rows = SAMPLE_BT * DEC_SEQ
    nseq = SAMPLE_BT
    x = x_ref[...]
    hn = _rms(x, gpre_ref[...]).astype(BF16)
    xy = _dot(hn, w_lru_ref[...])
    xl, yl = xy[:, :D_LRU], xy[:, D_LRU:]
    qk = _dot(hn, w_qk_ref[...])
    q, k = qk[:, :D_QK] * (GLA_DK ** -0.5), qk[:, D_QK:]
    v = _dot(hn, w_v_ref[...])
    g = _dot(hn, w_g_ref[...])
    gd = _dot(hn, w_gd_ref[...])

    t8 = _row_in_seg(xl.shape, DEC_SEQ)
    tail = tail_ref[...]
    xc = _causal_conv(
        xl, convw_ref, vec_ref[0:1, :],
        lambda kk: jnp.where(t8 < kk, _shift_down(tail, kk, in_group=True),
                             _shift_down(xl, kk, in_group=True)))
    xl_ref[...] = xl

    a, mult, gi = _lru_gates(xc, vec_ref, wa_ref, wx_ref)
    bterm = mult * gi * xc + a * h0_ref[...]
    _, hseq = _scan_affine8(a, bterm)
    hseq_ref[...] = hseq
    lru_out = _rms(hseq * jax.nn.gelu(yl), vec_ref[4:5, :])

    gk = _gk_from(gd, wgate_ref, bgate_ref)
    qs, ks, qe, ke, bl = _gla_operands(q, k, gk, DEC_SEQ, F32)
    masks = _level_masks(3)
    vb = v.astype(BF16)
    zeros_half = jnp.zeros((nseq, GLA_DK, GLA_DV), F32)

    def per_seq(x2):
        return x2.reshape(nseq, DEC_SEQ, x2.shape[1]).astype(BF16)

    decs = []
    for grp in range(GLA_HEADS // 2):
        bl3 = bl[:, LANES * grp:LANES * (grp + 1)].reshape(nseq, DEC_SEQ, LANES)
        decs.append([_decay_rows(bl3[b, 0:1, :]) for b in range(nseq)])
    outs = []
    for h in range(GLA_HEADS):
        grp, half = divmod(h, 2)
        lanes = slice(LANES * grp, LANES * (grp + 1))
        cols = slice(GLA_DV * h, GLA_DV * (h + 1))
        ks_rows = slice(GLA_DK * half, GLA_DK * (half + 1))
        p = _scores(qs, ks, masks, h)
        vh = vb[:, cols].reshape(rows // BLK, BLK, GLA_DV)
        o_intra = _bdot('cts,csv->ctv', p, vh).reshape(rows, GLA_DV)
        s0 = s0_ref[:, h]
        s_pad = jnp.concatenate([zeros_half, s0] if half else [s0, zeros_half], axis=1)
        o_inter = _bdot('btd,bdv->btv', per_seq(qe[half][:, lanes]), s_pad.astype(BF16))
        outs.append(o_intra + o_inter.reshape(rows, GLA_DV))
        upd = _bdot('bsd,bsv->bdv', per_seq(ke[half][:, lanes]), per_seq(v[:, cols]))
        for b in range(nseq):
            snew_ref[b, h] = decs[grp][b][ks_rows] * s0[b] + upd[b, ks_rows, :]
    gla_out = _gla_finish(jnp.concatenate(outs, axis=1), g, gn_ref[...])

    o_ref[...] = _out_proj(x, lru_out, gla_out, w_out_ref, gpost_ref)


def _sample_mixer_call(x_all, weights, h0pad, tailpad, s0, snew_all, l):
    rows = SAMPLE_BT * DEC_SEQ
    nb = DEC_BATCH // SAMPLE_BT
    base = N_PROMPT // rows
    const = lambda *ix: (lambda i: ix)
    st_spec = pl.BlockSpec((None, SAMPLE_BT, GLA_HEADS, GLA_DK, GLA_DV),
                           lambda i: (l, i, 0, 0, 0))
    out_shapes = (
        jax.ShapeDtypeStruct((N_ROWS, D_MODEL), F32),
        jax.ShapeDtypeStruct((N_SAMPLE, D_LRU), F32),
        jax.ShapeDtypeStruct((N_SAMPLE, D_LRU), F32),
        jax.ShapeDtypeStruct((DEPTH, DEC_BATCH, GLA_HEADS, GLA_DK, GLA_DV), F32),
    )
    out_specs = (
        pl.BlockSpec((rows, D_MODEL), lambda i: (base + i, 0)),
        pl.BlockSpec((rows, D_LRU), lambda i: (i, 0)),
        pl.BlockSpec((rows, D_LRU), lambda i: (i, 0)),
        st_spec,
    )
    in_specs = ([pl.BlockSpec((rows, D_MODEL), lambda i: (base + i, 0))]
                + _weight_specs(l, const)
                + [pl.BlockSpec((None, rows, D_LRU), lambda i: (l, i, 0)),
                   pl.BlockSpec((None, rows, D_LRU), lambda i: (l, i, 0)),
                   st_spec])
    args = [x_all, *weights, h0pad, tailpad, s0]
    aliases = {0: 0}
    if snew_all is not None:
        aliases[len(args)] = 3
        in_specs.append(pl.BlockSpec(memory_space=pl.ANY))
        args.append(snew_all)
    return pl.pallas_call(
        functools.partial(_sample_mixer_kernel, carried=snew_all is not None),
        grid=(nb,),
        in_specs=in_specs,
        out_specs=out_specs,
        out_shape=out_shapes,
        input_output_aliases=aliases,
        compiler_params=pltpu.CompilerParams(
            dimension_semantics=("parallel",), vmem_limit_bytes=VMEM_LIMIT),
        name="mixer_sample",
    )(*args)


def _blockdiag(w):
    per = MXU_DIM // LRU_BLOCK
    w = w.reshape(DEPTH, N_LRU_BLOCKS // per, per, LRU_BLOCK, LRU_BLOCK)
    eye = jnp.eye(per, dtype=w.dtype)
    bd = jnp.einsum('lpiab,ij->lpiajb', w, eye)
    return bd.reshape(DEPTH, N_LRU_BLOCKS // per, MXU_DIM, MXU_DIM)


def kernel(x_prompt, x_sample, state_lru_h, state_lru_conv, state_gla, norm_pre, norm_post, w_ffn_gate, w_ffn_up, w_ffn_down, w_in, conv_w, conv_b, lru_wa, lru_ba, lru_wx, lru_bx, lru_lambda, lru_norm, gla_w_gate, gla_b_gate, gla_norm, w_out):
    gpre4 = norm_pre.reshape(DEPTH, 3, 1, D_MODEL)
    gpost4 = norm_post.reshape(DEPTH, 3, 1, D_MODEL)
    w_ffn_f32 = (w_ffn_gate, w_ffn_up, w_ffn_down)
    vec = jnp.stack([conv_b, lru_ba, lru_bx, lru_lambda, lru_norm], axis=1)
    w_gd = w_in[:, :, D_IN - GLA_RANK:].astype(BF16)
    small = (conv_w, vec,
             _blockdiag(lru_wa).astype(BF16), _blockdiag(lru_wx).astype(BF16),
             gla_w_gate.astype(BF16), gla_b_gate.reshape(DEPTH, 1, D_QK),
             gla_norm.reshape(DEPTH, 1, GLA_DV))

    def mixer_weights(w_in_l, w_out_l):
        return (gpre4, gpost4, w_in_l, w_in_l, w_in_l, w_in_l, w_gd) + small + (w_out_l,)
    h0pad = jnp.pad(state_lru_h[:, :, None, :],
                    ((0, 0), (0, 0), (0, DEC_SEQ - 1), (0, 0))).reshape(DEPTH, N_SAMPLE, D_LRU)
    tailpad = jnp.pad(state_lru_conv,
                      ((0, 0), (0, 0), (SUBLANES - (CONV_W - 1), 0), (0, 0))
                      ).reshape(DEPTH, N_SAMPLE, D_LRU)

    w_bf16 = tuple(w[0, 0].astype(BF16) for w in w_ffn_f32)
    ffn_order = [(l, idx) for l in range(DEPTH) for idx in range(2)]

    def ffn(xs, l, idx, first=False, last=False):
        pos = ffn_order.index((l, idx))
        nxt = ffn_order[pos + 1] if pos + 1 < len(ffn_order) else None
        return _ffn_call(xs, gpre4, gpost4, w_bf16, w_ffn_f32 if nxt else None,
                         (w_in, w_out) if idx == 0 else None, l, idx, nxt, "ab"[idx], first, last)

    hp, cp, sp, hs, cs = [], [], [], [], []
    snew_all = None
    xs = (x_prompt.reshape(N_PROMPT, D_MODEL), x_sample.reshape(N_SAMPLE, D_MODEL))
    for l in range(DEPTH):
        (x_all,), w_bf16, mix_bf16 = ffn(xs, l, 0, first=(l == 0))
        weights = mixer_weights(*mix_bf16)
        x_all, hl, ct, sf = _prompt_mixer_call(x_all, weights, l)
        hp.append(hl[:, SUBLANES - 1])
        cp.append(ct[:, SUBLANES - (CONV_W - 1):])
        sp.append(sf)
        x_all, hseq, xl, snew_all = _sample_mixer_call(
            x_all, weights, h0pad, tailpad, state_gla, snew_all, l)
        hs.append(hseq.reshape(DEC_BATCH, DEC_SEQ, D_LRU)[:, DEC_SEQ - 1])
        cs.append(xl.reshape(DEC_BATCH, DEC_SEQ, D_LRU)[:, DEC_SEQ - (CONV_W - 1):])
        xs, w_bf16, _ = ffn((x_all,), l, 1, last=(l == DEPTH - 1))

    yp = xs[0].reshape(BATCH, SEQ, D_MODEL)
    ys = xs[1].reshape(DEC_BATCH, DEC_SEQ, D_MODEL)
    return (yp, ys, jnp.stack(hp), jnp.stack(cp), jnp.stack(sp),
            jnp.stack(hs), jnp.stack(cs), snew_all)
```

```python
import functools

import jax
import jax.numpy as jnp
from jax import lax
from jax.experimental import pallas as pl
from jax.experimental.pallas import tpu as pltpu

D_MODEL = 1024
BATCH = 8
SEQ = 2048
DEPTH = 4
DEC_BATCH = 128
DEC_SEQ = 8
D_LRU = 512
N_LRU_BLOCKS = 8
LRU_BLOCK = 64
CONV_W = 4
LRU_C = 8.0
D_GLA = 512
GLA_HEADS = 4
GLA_DV = 128
GLA_DK = 64
GLA_RANK = 16
GLA_NORMALIZER = 16.0
GLA_CHUNK = 64
D_FF = 2816
EPS = 1e-6
D_QK = GLA_HEADS * GLA_DK
D_IN = 2 * D_LRU + 2 * D_QK + 2 * D_GLA + GLA_RANK

N_PROMPT = BATCH * SEQ
N_SAMPLE = DEC_BATCH * DEC_SEQ
N_ROWS = N_PROMPT + N_SAMPLE

SUBLANES = 8
LANES = 128
MXU_DIM = 256
VMEM_LIMIT = 56 * 1024 * 1024

FFN_TM = 1024
FFN_TM_FIRST = 512
FFN_FC = 256
PROMPT_TT = 512
SAMPLE_BT = 32
BLK = 64
GLA_FAST_LIMIT = 40.0

F32 = jnp.float32
BF16 = jnp.bfloat16


def _rms(x, g):
    ms = jnp.mean(x * x, axis=-1, keepdims=True)
    return x * lax.rsqrt(ms + EPS) * g


def _softplus(x):
    return jnp.maximum(x, 0.0) + jnp.log(1.0 + jnp.exp(-jnp.abs(x)))


def _dot(a, b):
    return jnp.dot(a, b, preferred_element_type=F32)


def _dot_nt(a, b):
    return lax.dot_general(a, b, (((1,), (1,)), ((), ())), preferred_element_type=F32)


def _bdot(spec, a, b):
    return jnp.einsum(spec, a, b, preferred_element_type=F32)


N_FC = D_FF // FFN_FC
MIX_IN_CHUNKS = 7
MIX_IN_ROWS = D_IN // MIX_IN_CHUNKS
MIX_OUT_CHUNKS = 8
MIX_OUT_ROWS = D_MODEL // MIX_OUT_CHUNKS


def _ffn_kernel(*refs, first, last, convert, mix, np_tiles):
    it = iter(refs)
    x_refs = (next(it), next(it)) if first else (next(it),)
    gpre_ref, gpost_ref, wg_ref, wu_ref, wd_ref = (next(it) for _ in range(5))
    nxt = tuple(next(it) for _ in range(3)) if convert else ()
    mix_in = tuple(next(it) for _ in range(2)) if mix else ()
    o_refs = (next(it), next(it)) if last else (next(it),)
    cvt = tuple(next(it) for _ in range(3)) if convert else ()
    mix_out = tuple(next(it) for _ in range(2)) if mix else ()
    i = pl.program_id(0)

    if first:
        x_sc = next(it)

        @pl.when(i < np_tiles)
        def _():
            x_sc[...] = x_refs[0][...]

        @pl.when(i >= np_tiles)
        def _():
            x_sc[...] = x_refs[1][...]

        x = x_sc[...]
    else:
        x = x_refs[0][...]

    h = _rms(x, gpre_ref[...]).astype(BF16)
    acc = jnp.zeros((x.shape[0], D_MODEL), F32)
    for c0 in range(0, D_FF, FFN_FC):
        g = _dot(h, wg_ref[:, c0:c0 + FFN_FC])
        u = _dot(h, wu_ref[:, c0:c0 + FFN_FC])
        act = (jax.nn.silu(g) * u).astype(BF16)
        acc = acc + _dot(act, wd_ref[c0:c0 + FFN_FC, :])
    out = x + 0.5 * _rms(acc, gpost_ref[...])

    if last:
        @pl.when(i < np_tiles)
        def _():
            o_refs[0][...] = out

        @pl.when(i >= np_tiles)
        def _():
            o_refs[1][...] = out
    else:
        o_refs[0][...] = out

    if convert:
        @pl.when(i < N_FC)
        def _():
            for src, dst in zip(nxt, cvt):
                dst[...] = src[...].astype(BF16)

    if mix:
        @pl.when(i < MIX_IN_CHUNKS)
        def _():
            mix_out[0][...] = mix_in[0][...].astype(BF16)

        @pl.when(i < MIX_OUT_CHUNKS)
        def _():
            mix_out[1][...] = mix_in[1][...].astype(BF16)


def _ffn_call(xs, gpre4, gpost4, w_bf16, w_f32_next, mix_f32, l, idx, nxt_l_idx, sub, first,
              last):
    convert = w_f32_next is not None
    mix = mix_f32 is not None
    const = lambda *ix: (lambda i: ix)
    nidx = 2 * idx
    tm = FFN_TM_FIRST if first else FFN_TM
    np_tiles = N_PROMPT // tm
    prompt_ix = lambda i: (jnp.minimum(i, np_tiles - 1), 0)
    sample_ix = lambda i: (jnp.maximum(i - np_tiles, 0), 0)
    row_spec = lambda ix: pl.BlockSpec((tm, D_MODEL), ix)
    chunk = lambda i: jnp.minimum(i, N_FC - 1)

    in_specs = [row_spec(prompt_ix), row_spec(sample_ix)] if first else [row_spec(lambda i: (i, 0))]
    in_specs += [
        pl.BlockSpec((None, None, 1, D_MODEL), const(l, nidx, 0, 0)),
        pl.BlockSpec((None, None, 1, D_MODEL), const(l, nidx, 0, 0)),
        pl.BlockSpec((D_MODEL, D_FF), const(0, 0), pipeline_mode=pl.Buffered(1)),
        pl.BlockSpec((D_MODEL, D_FF), const(0, 0), pipeline_mode=pl.Buffered(1)),
        pl.BlockSpec((D_FF, D_MODEL), const(0, 0), pipeline_mode=pl.Buffered(1)),
    ]
    args = list(xs) + [gpre4, gpost4] + list(w_bf16)
    if convert:
        nl, ni = nxt_l_idx
        in_specs += [
            pl.BlockSpec((None, None, D_MODEL, FFN_FC), lambda i: (nl, ni, 0, chunk(i))),
            pl.BlockSpec((None, None, D_MODEL, FFN_FC), lambda i: (nl, ni, 0, chunk(i))),
            pl.BlockSpec((None, None, FFN_FC, D_MODEL), lambda i: (nl, ni, chunk(i), 0)),
        ]
        args += list(w_f32_next)
    in_chunk = lambda i: jnp.minimum(i, MIX_IN_CHUNKS - 1)
    out_chunk = lambda i: jnp.minimum(i, MIX_OUT_CHUNKS - 1)
    if mix:
        in_specs += [
            pl.BlockSpec((None, MIX_IN_ROWS, D_MODEL), lambda i: (l, in_chunk(i), 0)),
            pl.BlockSpec((None, MIX_OUT_ROWS, D_MODEL), lambda i: (l, out_chunk(i), 0)),
        ]
        args += list(mix_f32)

    if last:
        out_specs = [row_spec(prompt_ix), row_spec(sample_ix)]
        out_shape = [jax.ShapeDtypeStruct((N_PROMPT, D_MODEL), F32),
                     jax.ShapeDtypeStruct((N_SAMPLE, D_MODEL), F32)]
    else:
        out_specs = [row_spec(lambda i: (i, 0))]
        out_shape = [jax.ShapeDtypeStruct((N_ROWS, D_MODEL), F32)]
    n_act = len(out_specs)
    if convert:
        out_specs += [
            pl.BlockSpec((D_MODEL, FFN_FC), lambda i: (0, chunk(i))),
            pl.BlockSpec((D_MODEL, FFN_FC), lambda i: (0, chunk(i))),
            pl.BlockSpec((FFN_FC, D_MODEL), lambda i: (chunk(i), 0)),
        ]
        out_shape += [jax.ShapeDtypeStruct((D_MODEL, D_FF), BF16),
                      jax.ShapeDtypeStruct((D_MODEL, D_FF), BF16),
                      jax.ShapeDtypeStruct((D_FF, D_MODEL), BF16)]
    n_cvt = len(out_specs)
    if mix:
        out_specs += [
            pl.BlockSpec((MIX_IN_ROWS, D_MODEL), lambda i: (in_chunk(i), 0)),
            pl.BlockSpec((MIX_OUT_ROWS, D_MODEL), lambda i: (out_chunk(i), 0)),
        ]
        out_shape += [jax.ShapeDtypeStruct((D_IN, D_MODEL), BF16),
                      jax.ShapeDtypeStruct((D_MODEL, D_MODEL), BF16)]

    res = pl.pallas_call(
        functools.partial(_ffn_kernel, first=first, last=last, convert=convert, mix=mix,
                          np_tiles=np_tiles),
        grid=(N_ROWS // tm,),
        in_specs=in_specs,
        out_specs=out_specs,
        out_shape=out_shape,
        scratch_shapes=[pltpu.VMEM((tm, D_MODEL), F32)] if first else [],
        input_output_aliases={} if (first or last) else {0: 0},
        compiler_params=pltpu.CompilerParams(
            dimension_semantics=("arbitrary",), vmem_limit_bytes=VMEM_LIMIT),
        name=f"ffn_{sub}",
    )(*args)
    return res[:n_act], tuple(res[n_act:n_cvt]), tuple(res[n_cvt:])


def _grouped(x):
    return x.reshape(x.shape[0] // SUBLANES, SUBLANES, x.shape[1])


def _sub_iota(x3):
    return lax.broadcasted_iota(jnp.int32, x3.shape, 1)


def _groups_down(x3, q, first=None):
    if q == 0:
        return x3
    head = jnp.zeros((q,) + x3.shape[1:], x3.dtype) if first is None else first
    return jnp.concatenate([head, x3[:x3.shape[0] - q]], axis=0)


def _groups_up(x3, q):
    if q == 0:
        return x3
    return jnp.concatenate([x3[q:], jnp.zeros((q,) + x3.shape[1:], x3.dtype)], axis=0)


def _shift_down(x, k, first=None, in_group=False):
    x3 = _grouped(x)
    q, r = divmod(k, SUBLANES)
    if r == 0:
        y = _groups_down(x3, q)
    else:
        xr = pltpu.roll(x3, r, 1)
        if in_group:
            y = xr
        else:
            fr = None if first is None else pltpu.roll(_grouped(first), r, 1)
            y = jnp.where(_sub_iota(x3) < r, _groups_down(xr, q + 1, fr if q == 0 else None),
                          _groups_down(xr, q))
    return y.reshape(x.shape)


def _shift_up(x, k, in_group=False):
    x3 = _grouped(x)
    q, r = divmod(k, SUBLANES)
    if r == 0:
        y = _groups_up(x3, q)
    else:
        xr = pltpu.roll(x3, SUBLANES - r, 1)
        if in_group:
            y = xr
        else:
            y = jnp.where(_sub_iota(x3) >= SUBLANES - r, _groups_up(xr, q + 1), _groups_up(xr, q))
    return y.reshape(x.shape)


def _row_in_seg(shape, seg):
    return lax.broadcasted_iota(jnp.int32, shape, 0) & (seg - 1)


def _scan_affine8(a, b):
    t = _row_in_seg(a.shape, SUBLANES)
    d = 1
    while d < SUBLANES:
        m = t >= d
        b = jnp.where(m, a * _shift_down(b, d, in_group=True) + b, b)
        a = jnp.where(m, a * _shift_down(a, d, in_group=True), a)
        d *= 2
    return a, b


def _seg_cumsum(x, seg):
    t = _row_in_seg(x.shape, seg)
    d = 1
    while d < seg:
        x = jnp.where(t >= d, x + _shift_down(x, d, in_group=(seg == SUBLANES)), x)
        d *= 2
    return x


def _causal_conv(xl, convw_ref, bias, shifted):
    xc = None
    for jj in range(CONV_W):
        kk = CONV_W - 1 - jj
        term = (xl if kk == 0 else shifted(kk)) * convw_ref[jj:jj + 1, :]
        xc = term if xc is None else xc + term
    return xc + bias


def _lru_gates(xc, vec_ref, wa_ref, wx_ref):
    half = D_LRU // 2
    xb = xc.astype(BF16)
    xb0, xb1 = xb[:, :half], xb[:, half:]
    ba, bx, lam = vec_ref[1:2, :], vec_ref[2:3, :], vec_ref[3:4, :]
    r = jax.nn.sigmoid(
        jnp.concatenate([_dot(xb0, wa_ref[0]), _dot(xb1, wa_ref[1])], axis=1) + ba)
    i = jax.nn.sigmoid(
        jnp.concatenate([_dot(xb0, wx_ref[0]), _dot(xb1, wx_ref[1])], axis=1) + bx)
    log_a = -LRU_C * r * _softplus(-lam)
    a = jnp.exp(log_a)
    mult = jnp.sqrt(-jnp.tanh(log_a) * (a * a + 1.0))
    return a, mult, i


def _lru_carry(a8, h8, hin):
    hs = []
    for gidx in range(a8.shape[0] // SUBLANES):
        rows = slice(SUBLANES * gidx, SUBLANES * (gidx + 1))
        hg = a8[rows] * hin + h8[rows]
        hin = hg[SUBLANES - 1:SUBLANES, :]
        hs.append(hg)
    return hs


def _level_masks(nlev):
    t = lax.broadcasted_iota(jnp.int32, (1, BLK, BLK), 1)
    s = lax.broadcasted_iota(jnp.int32, (1, BLK, BLK), 2)
    masks = [t == s]
    for lv in range(nlev):
        masks.append((((t >> lv) ^ (s >> lv)) == 1) & (((t >> lv) & 1) == 1))
    return masks


def _head_halves(val, dtype):
    second = (lax.broadcasted_iota(jnp.int32, val.shape, 1) & (LANES - 1)) >= GLA_DK
    zero = jnp.zeros_like(val)
    return (jnp.where(second, zero, val).astype(dtype), jnp.where(second, val, zero).astype(dtype))


def _gla_operands(q, k, gk, seg, state_dtype):
    t = lax.broadcasted_iota(jnp.int32, gk.shape, 0)
    b = _seg_cumsum(gk, seg)
    qs = [_head_halves(q, BF16)]
    ks = [k.astype(BF16)]
    ref = b - gk
    end = b
    s = 1
    while s < seg:
        qs.append(_head_halves(q * jnp.exp(b - ref), BF16))
        ks.append((k * jnp.exp(end - b)).astype(BF16))
        bit = (t & s) != 0
        ref = jnp.where(bit, _shift_down(ref, s, in_group=True), ref)
        end = jnp.where(bit, end, _shift_up(end, s, in_group=True))
        s *= 2
    qe = _head_halves(q * jnp.exp(b), state_dtype)
    ke = _head_halves(k * jnp.exp(end - b), state_dtype)
    return qs, ks, qe, ke, end


def _gla_operands_bounded(q, k, gk):
    n, c = gk.shape
    b = _seg_cumsum(gk, GLA_CHUNK)
    b3 = b.reshape(n // GLA_CHUNK, GLA_CHUNK, c)
    total = jnp.broadcast_to(b3[:, GLA_CHUNK - 1:GLA_CHUNK, :], b3.shape).reshape(n, c)
    qe = _head_halves(q * jnp.exp(b), BF16)
    ke = _head_halves(k * jnp.exp(total - b), BF16)
    kf = (k * jnp.exp(-b)).astype(BF16)
    return qe, ke, kf, total


def _chunk_decay_floor(gk):
    n, c = gk.shape
    return jnp.min(jnp.sum(gk.reshape(n // GLA_CHUNK, GLA_CHUNK, c), axis=1))


def _blocks(x, lanes):
    return x[:, lanes].reshape(x.shape[0] // BLK, BLK, LANES)


def _scores_bounded(qe, kf, h):
    grp, half = divmod(h, 2)
    lanes = slice(LANES * grp, LANES * (grp + 1))
    t = lax.broadcasted_iota(jnp.int32, (1, BLK, BLK), 1)
    s = lax.broadcasted_iota(jnp.int32, (1, BLK, BLK), 2)
    sc = _bdot('ctd,csd->cts', _blocks(qe[half], lanes), _blocks(kf, lanes))
    return jnp.where(t >= s, sc, 0.0).astype(BF16)


def _scores(qs, ks, masks, h):
    grp, half = divmod(h, 2)
    lanes = slice(LANES * grp, LANES * (grp + 1))
    p = None
    for lv, m in enumerate(masks):
        sc = _bdot('ctd,csd->cts', _blocks(qs[lv][half], lanes), _blocks(ks[lv], lanes))
        p = jnp.where(m, sc, 0.0 if p is None else p)
    return p.astype(BF16)


def _decay_rows(bl_row):
    return jnp.exp(jnp.transpose(jnp.broadcast_to(bl_row, (LANES, LANES))))


def _gla_finish(o, g, gn):
    outs = []
    for h in range(GLA_HEADS):
        sl = slice(GLA_DV * h, GLA_DV * (h + 1))
        outs.append(_rms(o[:, sl], gn) * jax.nn.silu(g[:, sl]))
    return jnp.concatenate(outs, axis=1)


def _gk_from(gd, wgate_ref, bgate_ref):
    z = _dot(gd.astype(BF16), wgate_ref[...]) + bgate_ref[...]
    return -_softplus(-z) * (1.0 / GLA_NORMALIZER)


def _out_proj(x, lru_out, gla_out, w_out_ref, gpost_ref):
    m = (_dot(lru_out.astype(BF16), w_out_ref[:D_LRU, :])
         + _dot(gla_out.astype(BF16), w_out_ref[D_LRU:, :]))
    return x + _rms(m, gpost_ref[...])


def _prompt_mixer_kernel(
        x_ref, gpre_ref, gpost_ref, w_lru_ref, w_qk_ref, w_v_ref, w_g_ref, w_gd_ref,
        convw_ref, vec_ref, wa_ref, wx_ref, wgate_ref, bgate_ref, gn_ref, w_out_ref,
        o_ref, hlast_ref, ctail_ref, sfin_ref,
        tail_sc, h_sc, s_sc, x_sc, xy_sc, qk_sc, gk_sc, v_sc, g_sc, bounded_sc):
    s = pl.program_id(0)
    tt = PROMPT_TT
    nt = SEQ // tt
    nchunk = tt // BLK
    seq_start = lax.rem(s + nt - 1, nt) == 0

    @pl.when(s == 0)
    def _():
        for ref in (x_sc, xy_sc, qk_sc, gk_sc, v_sc, g_sc):
            ref[...] = jnp.zeros_like(ref)
        bounded_sc[0] = 1

    @pl.when((s == 0) | seq_start)
    def _():
        tail_sc[...] = jnp.zeros_like(tail_sc)
        h_sc[...] = jnp.zeros_like(h_sc)
        s_sc[...] = jnp.zeros_like(s_sc)

    def body(bounded):
        x_new = x_ref[...]
        hn = _rms(x_new, gpre_ref[...]).astype(BF16)

        xy = xy_sc[...]
        xl, yl = xy[:, :D_LRU], xy[:, D_LRU:]
        tail = tail_sc[...]
        xc = _causal_conv(xl, convw_ref, vec_ref[0:1, :],
                          lambda kk: _shift_down(xl, kk, first=tail))
        tail_sc[...] = xl[tt - SUBLANES:, :]
        ctail_ref[...] = xl[tt - SUBLANES:, :]
        xy_sc[...] = _dot_nt(hn, w_lru_ref[...])
        a, mult, gi = _lru_gates(xc, vec_ref, wa_ref, wx_ref)

        qk = qk_sc[...]
        q, k = qk[:, :D_QK] * (GLA_DK ** -0.5), qk[:, D_QK:]
        gk = gk_sc[...]
        if bounded:
            qe, ke, kf, bl = _gla_operands_bounded(q, k, gk)
        else:
            qs, ks, qe, ke, bl = _gla_operands(q, k, gk, GLA_CHUNK, BF16)
            masks = _level_masks(6)
        qk_sc[...] = _dot_nt(hn, w_qk_ref[...])
        gk_new = _gk_from(_dot_nt(hn, w_gd_ref[...]), wgate_ref, bgate_ref)
        gk_sc[...] = gk_new
        bounded_sc[0] = (_chunk_decay_floor(gk_new) >= -GLA_FAST_LIMIT).astype(jnp.int32)

        first_row = (lax.broadcasted_iota(jnp.int32, a.shape, 0) == 0) & seq_start
        mult_r = jnp.where(first_row, 1.0, mult)
        a8, h8 = _scan_affine8(a, mult_r * gi * xc)
        hs = _lru_carry(a8, h8, h_sc[SUBLANES - 1:SUBLANES, :])
        h_sc[...] = hs[-1]
        hlast_ref[...] = hs[-1]
        lru_out = _rms(jnp.concatenate(hs, axis=0) * jax.nn.gelu(yl), vec_ref[4:5, :])

        vb = v_sc[...]
        ps, vhs, upds = [], [], []
        for h in range(GLA_HEADS):
            grp, half = divmod(h, 2)
            lanes = slice(LANES * grp, LANES * (grp + 1))
            ps.append(_scores_bounded(qe, kf, h) if bounded else _scores(qs, ks, masks, h))
            vhs.append(vb[:, GLA_DV * h:GLA_DV * (h + 1)].reshape(nchunk, BLK, GLA_DV))
            upds.append(_bdot('csd,csv->cdv', _blocks(ke[half], lanes), vhs[h]))
        v_sc[...] = _dot_nt(hn, w_v_ref[...]).astype(BF16)
        g_new = _dot_nt(hn, w_g_ref[...])

        decs = [[_decay_rows(bl[c * BLK:c * BLK + 1, LANES * grp:LANES * (grp + 1)])
                 for c in range(nchunk)] for grp in range(GLA_HEADS // 2)]
        outs = []
        for h in range(GLA_HEADS):
            grp, half = divmod(h, 2)
            lanes = slice(LANES * grp, LANES * (grp + 1))
            st = s_sc[h]
            states = []
            for c in range(nchunk):
                states.append(st.astype(BF16))
                st = decs[grp][c] * st + upds[h][c]
            s_sc[h] = st
            sfin_ref[h] = st[GLA_DK * half:GLA_DK * (half + 1), :]
            o = (_bdot('cts,csv->ctv', ps[h], vhs[h])
                 + _bdot('ctd,cdv->ctv', _blocks(qe[half], lanes), jnp.stack(states)))
            outs.append(o.reshape(tt, GLA_DV))
        gla_out = _gla_finish(jnp.concatenate(outs, axis=1), g_sc[...], gn_ref[...])
        g_sc[...] = g_new

        o_ref[...] = _out_proj(x_sc[...], lru_out, gla_out, w_out_ref, gpost_ref)
        x_sc[...] = x_new

    lax.cond(bounded_sc[0] != 0, functools.partial(body, True), functools.partial(body, False))


def _weight_specs(l, const):
    d = D_MODEL
    return [
        pl.BlockSpec((None, None, 1, d), const(l, 1, 0, 0)),
        pl.BlockSpec((None, None, 1, d), const(l, 1, 0, 0)),
        pl.BlockSpec((2 * D_LRU, d), const(0, 0)),
        pl.BlockSpec((2 * D_QK, d), const(2, 0)),
        pl.BlockSpec((D_GLA, d), const(3, 0)),
        pl.BlockSpec((D_GLA, d), const(4, 0)),
        pl.BlockSpec((GLA_RANK, d), const((D_IN - GLA_RANK) // GLA_RANK, 0)),
        pl.BlockSpec((None, CONV_W, D_LRU), const(l, 0, 0)),
        pl.BlockSpec((None, 5, D_LRU), const(l, 0, 0)),
        pl.BlockSpec((None, 2, MXU_DIM, MXU_DIM), const(l, 0, 0, 0)),
        pl.BlockSpec((None, 2, MXU_DIM, MXU_DIM), const(l, 0, 0, 0)),
        pl.BlockSpec((None, GLA_RANK, D_QK), const(l, 0, 0)),
        pl.BlockSpec((None, 1, D_QK), const(l, 0, 0)),
        pl.BlockSpec((None, 1, GLA_DV), const(l, 0, 0)),
        pl.BlockSpec((d, d), const(0, 0)),
    ]


def _prompt_mixer_call(x_all, weights, l):
    tt = PROMPT_TT
    nt = SEQ // tt
    ntiles = BATCH * nt
    const = lambda *ix: (lambda s: ix)
    done = lambda s: jnp.maximum(s - 1, 0)
    out_shapes = (
        jax.ShapeDtypeStruct((N_ROWS, D_MODEL), F32),
        jax.ShapeDtypeStruct((BATCH, SUBLANES, D_LRU), F32),
        jax.ShapeDtypeStruct((BATCH, SUBLANES, D_LRU), F32),
        jax.ShapeDtypeStruct((BATCH, GLA_HEADS, GLA_DK, GLA_DV), F32),
    )
    out_specs = (
        pl.BlockSpec((tt, D_MODEL), lambda s: (done(s), 0)),
        pl.BlockSpec((None, SUBLANES, D_LRU), lambda s: (done(s) // nt, 0, 0)),
        pl.BlockSpec((None, SUBLANES, D_LRU), lambda s: (done(s) // nt, 0, 0)),
        pl.BlockSpec((None, GLA_HEADS, GLA_DK, GLA_DV), lambda s: (done(s) // nt, 0, 0, 0)),
    )
    scratch = [
        pltpu.VMEM((SUBLANES, D_LRU), F32),
        pltpu.VMEM((SUBLANES, D_LRU), F32),
        pltpu.VMEM((GLA_HEADS, LANES, GLA_DV), F32),
        pltpu.VMEM((tt, D_MODEL), F32),
        pltpu.VMEM((tt, 2 * D_LRU), F32),
        pltpu.VMEM((tt, 2 * D_QK), F32),
        pltpu.VMEM((tt, D_QK), F32),
        pltpu.VMEM((tt, D_GLA), BF16),
        pltpu.VMEM((tt, D_GLA), F32),
        pltpu.SMEM((1,), jnp.int32),
    ]
    return pl.pallas_call(
        _prompt_mixer_kernel,
        grid=(ntiles + 1,),
        in_specs=[pl.BlockSpec((tt, D_MODEL), lambda s: (jnp.minimum(s, ntiles - 1), 0))]
                 + _weight_specs(l, const),
        out_specs=out_specs,
        out_shape=out_shapes,
        scratch_shapes=scratch,
        input_output_aliases={0: 0},
        compiler_params=pltpu.CompilerParams(
            dimension_semantics=("arbitrary",), vmem_limit_bytes=VMEM_LIMIT),
        name="mixer_prompt",
    )(x_all, *weights)


def _sample_mixer_kernel(
        x_ref, gpre_ref, gpost_ref, w_lru_ref, w_qk_ref, w_v_ref, w_g_ref, w_gd_ref,
        convw_ref, vec_ref, wa_ref, wx_ref, wgate_ref, bgate_ref, gn_ref, w_out_ref,
        h0_ref, tail_ref, s0_ref, *rest, carried):
    o_ref, hseq_ref, xl_ref, snew_ref = rest[1:] if carried else rest
    rows = SAMPLE_BT * DEC_SEQ
    nseq = SAMPLE_BT
    x = x_ref[...]
    hn = _rms(x, gpre_ref[...]).astype(BF16)
    xy = _dot_nt(hn, w_lru_ref[...])
    xl, yl = xy[:, :D_LRU], xy[:, D_LRU:]
    qk = _dot_nt(hn, w_qk_ref[...])
    q, k = qk[:, :D_QK] * (GLA_DK ** -0.5), qk[:, D_QK:]
    v = _dot_nt(hn, w_v_ref[...])
    g = _dot_nt(hn, w_g_ref[...])
    gd = _dot_nt(hn, w_gd_ref[...])

    t8 = _row_in_seg(xl.shape, DEC_SEQ)
    tail = tail_ref[...]
    xc = _causal_conv(
        xl, convw_ref, vec_ref[0:1, :],
        lambda kk: jnp.where(t8 < kk, _shift_down(tail, kk, in_group=True),
                             _shift_down(xl, kk, in_group=True)))
    xl_ref[...] = xl

    a, mult, gi = _lru_gates(xc, vec_ref, wa_ref, wx_ref)
    bterm = mult * gi * xc + a * h0_ref[...]
    _, hseq = _scan_affine8(a, bterm)
    hseq_ref[...] = hseq
    lru_out = _rms(hseq * jax.nn.gelu(yl), vec_ref[4:5, :])

    gk = _gk_from(gd, wgate_ref, bgate_ref)
    qs, ks, qe, ke, bl = _gla_operands(q, k, gk, DEC_SEQ, F32)
    masks = _level_masks(3)
    vb = v.astype(BF16)
    zeros_half = jnp.zeros((nseq, GLA_DK, GLA_DV), F32)

    def per_seq(x2):
        return x2.reshape(nseq, DEC_SEQ, x2.shape[1]).astype(BF16)

    decs = []
    for grp in range(GLA_HEADS // 2):
        bl3 = bl[:, LANES * grp:LANES * (grp + 1)].reshape(nseq, DEC_SEQ, LANES)
        decs.append([_decay_rows(bl3[b, 0:1, :]) for b in range(nseq)])
    outs = []
    for h in range(GLA_HEADS):
        grp, half = divmod(h, 2)
        lanes = slice(LANES * grp, LANES * (grp + 1))
        cols = slice(GLA_DV * h, GLA_DV * (h + 1))
        ks_rows = slice(GLA_DK * half, GLA_DK * (half + 1))
        p = _scores(qs, ks, masks, h)
        vh = vb[:, cols].reshape(rows // BLK, BLK, GLA_DV)
        o_intra = _bdot('cts,csv->ctv', p, vh).reshape(rows, GLA_DV)
        s0 = s0_ref[:, h]
        s_pad = jnp.concatenate([zeros_half, s0] if half else [s0, zeros_half], axis=1)
        o_inter = _bdot('btd,bdv->btv', per_seq(qe[half][:, lanes]), s_pad.astype(BF16))
        outs.append(o_intra + o_inter.reshape(rows, GLA_DV))
        upd = _bdot('bsd,bsv->bdv', per_seq(ke[half][:, lanes]), per_seq(v[:, cols]))
        for b in range(nseq):
            snew_ref[b, h] = decs[grp][b][ks_rows] * s0[b] + upd[b, ks_rows, :]
    gla_out = _gla_finish(jnp.concatenate(outs, axis=1), g, gn_ref[...])

    o_ref[...] = _out_proj(x, lru_out, gla_out, w_out_ref, gpost_ref)


def _sample_mixer_call(x_all, weights, h0pad, tailpad, s0, snew_all, l):
    rows = SAMPLE_BT * DEC_SEQ
    nb = DEC_BATCH // SAMPLE_BT
    base = N_PROMPT // rows
    const = lambda *ix: (lambda i: ix)
    st_spec = pl.BlockSpec((None, SAMPLE_BT, GLA_HEADS, GLA_DK, GLA_DV),
                           lambda i: (l, i, 0, 0, 0))
    out_shapes = (
        jax.ShapeDtypeStruct((N_ROWS, D_MODEL), F32),
        jax.ShapeDtypeStruct((N_SAMPLE, D_LRU), F32),
        jax.ShapeDtypeStruct((N_SAMPLE, D_LRU), F32),
        jax.ShapeDtypeStruct((DEPTH, DEC_BATCH, GLA_HEADS, GLA_DK, GLA_DV), F32),
    )
    out_specs = (
        pl.BlockSpec((rows, D_MODEL), lambda i: (base + i, 0)),
        pl.BlockSpec((rows, D_LRU), lambda i: (i, 0)),
        pl.BlockSpec((rows, D_LRU), lambda i: (i, 0)),
        st_spec,
    )
    in_specs = ([pl.BlockSpec((rows, D_MODEL), lambda i: (base + i, 0))]
                + _weight_specs(l, const)
                + [pl.BlockSpec((None, rows, D_LRU), lambda i: (l, i, 0)),
                   pl.BlockSpec((None, rows, D_LRU), lambda i: (l, i, 0)),
                   st_spec])
    args = [x_all, *weights, h0pad, tailpad, s0]
    aliases = {0: 0}
    if snew_all is not None:
        aliases[len(args)] = 3
        in_specs.append(pl.BlockSpec(memory_space=pl.ANY))
        args.append(snew_all)
    return pl.pallas_call(
        functools.partial(_sample_mixer_kernel, carried=snew_all is not None),
        grid=(nb,),
        in_specs=in_specs,
        out_specs=out_specs,
        out_shape=out_shapes,
        input_output_aliases=aliases,
        compiler_params=pltpu.CompilerParams(
            dimension_semantics=("parallel",), vmem_limit_bytes=VMEM_LIMIT),
        name="mixer_sample",
    )(*args)


def _blockdiag(w):
    per = MXU_DIM // LRU_BLOCK
    w = w.reshape(DEPTH, N_LRU_BLOCKS // per, per, LRU_BLOCK, LRU_BLOCK)
    eye = jnp.eye(per, dtype=w.dtype)
    bd = jnp.einsum('lpiab,ij->lpiajb', w, eye)
    return bd.reshape(DEPTH, N_LRU_BLOCKS // per, MXU_DIM, MXU_DIM)


def kernel(x_prompt, x_sample, state_lru_h, state_lru_conv, state_gla, norm_pre, norm_post, w_ffn_gate, w_ffn_up, w_ffn_down, w_in, conv_w, conv_b, lru_wa, lru_ba, lru_wx, lru_bx, lru_lambda, lru_norm, gla_w_gate, gla_b_gate, gla_norm, w_out):
    gpre4 = norm_pre.reshape(DEPTH, 3, 1, D_MODEL)
    gpost4 = norm_post.reshape(DEPTH, 3, 1, D_MODEL)
    w_ffn_f32 = (w_ffn_gate, w_ffn_up, w_ffn_down)
    vec = jnp.stack([conv_b, lru_ba, lru_bx, lru_lambda, lru_norm], axis=1)
    w_in_t = jnp.swapaxes(w_in, 1, 2)
    small = (conv_w, vec,
             _blockdiag(lru_wa).astype(BF16), _blockdiag(lru_wx).astype(BF16),
             gla_w_gate.astype(BF16), gla_b_gate.reshape(DEPTH, 1, D_QK),
             gla_norm.reshape(DEPTH, 1, GLA_DV))

    def mixer_weights(w_in_l, w_out_l):
        return (gpre4, gpost4) + (w_in_l,) * 5 + small + (w_out_l,)
    h0pad = jnp.pad(state_lru_h[:, :, None, :],
                    ((0, 0), (0, 0), (0, DEC_SEQ - 1), (0, 0))).reshape(DEPTH, N_SAMPLE, D_LRU)
    tailpad = jnp.pad(state_lru_conv,
                      ((0, 0), (0, 0), (SUBLANES - (CONV_W - 1), 0), (0, 0))
                      ).reshape(DEPTH, N_SAMPLE, D_LRU)

    w_bf16 = tuple(w[0, 0].astype(BF16) for w in w_ffn_f32)
    ffn_order = [(l, idx) for l in range(DEPTH) for idx in range(2)]

    def ffn(xs, l, idx, first=False, last=False):
        pos = ffn_order.index((l, idx))
        nxt = ffn_order[pos + 1] if pos + 1 < len(ffn_order) else None
        return _ffn_call(xs, gpre4, gpost4, w_bf16, w_ffn_f32 if nxt else None,
                         (w_in_t, w_out) if idx == 0 else None, l, idx, nxt, "ab"[idx], first,
                         last)

    hp, cp, sp, hs, cs = [], [], [], [], []
    snew_all = None
    xs = (x_prompt.reshape(N_PROMPT, D_MODEL), x_sample.reshape(N_SAMPLE, D_MODEL))
    for l in range(DEPTH):
        (x_all,), w_bf16, mix_bf16 = ffn(xs, l, 0, first=(l == 0))
        weights = mixer_weights(*mix_bf16)
        x_all, hl, ct, sf = _prompt_mixer_call(x_all, weights, l)
        hp.append(hl[:, SUBLANES - 1])
        cp.append(ct[:, SUBLANES - (CONV_W - 1):])
        sp.append(sf)
        x_all, hseq, xl, snew_all = _sample_mixer_call(
            x_all, weights, h0pad, tailpad, state_gla, snew_all, l)
        hs.append(hseq.reshape(DEC_BATCH, DEC_SEQ, D_LRU)[:, DEC_SEQ - 1])
        cs.append(xl.reshape(DEC_BATCH, DEC_SEQ, D_LRU)[:, DEC_SEQ - (CONV_W - 1):])
        xs, w_bf16, _ = ffn((x_all,), l, 1, last=(l == DEPTH - 1))

    yp = xs[0].reshape(BATCH, SEQ, D_MODEL)
    ys = xs[1].reshape(DEC_BATCH, DEC_SEQ, D_MODEL)
    return (yp, ys, jnp.stack(hp), jnp.stack(cp), jnp.stack(sp),
            jnp.stack(hs), jnp.stack(cs), snew_all)
```

```python
import functools

import jax
import jax.numpy as jnp
from jax import lax
from jax.experimental import pallas as pl
from jax.experimental.pallas import tpu as pltpu

D_MODEL = 1024
BATCH = 8
SEQ = 2048
DEPTH = 4
DEC_BATCH = 128
DEC_SEQ = 8
D_LRU = 512
N_LRU_BLOCKS = 8
LRU_BLOCK = 64
CONV_W = 4
LRU_C = 8.0
D_GLA = 512
GLA_HEADS = 4
GLA_DV = 128
GLA_DK = 64
GLA_RANK = 16
GLA_NORMALIZER = 16.0
GLA_CHUNK = 64
D_FF = 2816
EPS = 1e-6
D_QK = GLA_HEADS * GLA_DK
D_IN = 2 * D_LRU + 2 * D_QK + 2 * D_GLA + GLA_RANK

N_PROMPT = BATCH * SEQ
N_SAMPLE = DEC_BATCH * DEC_SEQ
N_ROWS = N_PROMPT + N_SAMPLE

SUBLANES = 8
LANES = 128
MXU_DIM = 256
VMEM_LIMIT = 56 * 1024 * 1024

FFN_TM = 1024
FFN_TM_FIRST = 512
FFN_FC = 256
FFN_MM = 256
PROMPT_TT = 512
SAMPLE_BT = 32
BLK = 64
GLA_FAST_LIMIT = 40.0

F32 = jnp.float32
BF16 = jnp.bfloat16


def _rms(x, g):
    ms = jnp.mean(x * x, axis=-1, keepdims=True)
    return x * lax.rsqrt(ms + EPS) * g


def _softplus(x):
    return jnp.maximum(x, 0.0) + jnp.log(1.0 + jnp.exp(-jnp.abs(x)))


def _dot(a, b):
    return jnp.dot(a, b, preferred_element_type=F32)


def _dot_nt(a, b):
    return lax.dot_general(a, b, (((1,), (1,)), ((), ())), preferred_element_type=F32)


def _bdot(spec, a, b):
    return jnp.einsum(spec, a, b, preferred_element_type=F32)


N_FC = D_FF // FFN_FC
MIX_IN_CHUNKS = 7
MIX_IN_ROWS = D_IN // MIX_IN_CHUNKS
MIX_OUT_CHUNKS = 8
MIX_OUT_ROWS = D_MODEL // MIX_OUT_CHUNKS


def _ffn_kernel(*refs, first, last, convert, mix, np_tiles):
    it = iter(refs)
    x_refs = (next(it), next(it)) if first else (next(it),)
    gpre_ref, gpost_ref, wg_ref, wu_ref, wd_ref = (next(it) for _ in range(5))
    nxt = tuple(next(it) for _ in range(3)) if convert else ()
    mix_in = tuple(next(it) for _ in range(2)) if mix else ()
    o_refs = (next(it), next(it)) if last else (next(it),)
    cvt = tuple(next(it) for _ in range(3)) if convert else ()
    mix_out = tuple(next(it) for _ in range(2)) if mix else ()
    i = pl.program_id(0)

    if first:
        x_sc = next(it)

        @pl.when(i < np_tiles)
        def _():
            x_sc[...] = x_refs[0][...]

        @pl.when(i >= np_tiles)
        def _():
            x_sc[...] = x_refs[1][...]

        x = x_sc[...]
    else:
        x = x_refs[0][...]

    h = _rms(x, gpre_ref[...]).astype(BF16)
    bounds = list(range(0, D_FF, FFN_MM)) + [D_FF]
    spans = list(zip(bounds[:-1], bounds[1:]))

    def gate_up(span):
        return _dot(h, wg_ref[:, span[0]:span[1]]), _dot(h, wu_ref[:, span[0]:span[1]])

    acc = jnp.zeros((x.shape[0], D_MODEL), F32)
    g, u = gate_up(spans[0])
    for n, span in enumerate(spans):
        g_next, u_next = gate_up(spans[n + 1]) if n + 1 < len(spans) else (None, None)
        act = (jax.nn.silu(g) * u).astype(BF16)
        acc = acc + _dot(act, wd_ref[span[0]:span[1], :])
        g, u = g_next, u_next
    out = x + _rms(acc, 0.5 * gpost_ref[...])

    if last:
        @pl.when(i < np_tiles)
        def _():
            o_refs[0][...] = out

        @pl.when(i >= np_tiles)
        def _():
            o_refs[1][...] = out
    else:
        o_refs[0][...] = out

    if convert:
        @pl.when(i < N_FC)
        def _():
            for src, dst in zip(nxt, cvt):
                dst[...] = src[...].astype(BF16)

    if mix:
        @pl.when(i < MIX_IN_CHUNKS)
        def _():
            mix_out[0][...] = mix_in[0][...].astype(BF16)

        @pl.when(i < MIX_OUT_CHUNKS)
        def _():
            mix_out[1][...] = mix_in[1][...].astype(BF16)


def _ffn_call(xs, gpre4, gpost4, w_bf16, w_f32_next, mix_f32, l, idx, nxt_l_idx, sub, first,
              last):
    convert = w_f32_next is not None
    mix = mix_f32 is not None
    const = lambda *ix: (lambda i: ix)
    nidx = 2 * idx
    tm = FFN_TM_FIRST if first else FFN_TM
    np_tiles = N_PROMPT // tm
    prompt_ix = lambda i: (jnp.minimum(i, np_tiles - 1), 0)
    sample_ix = lambda i: (jnp.maximum(i - np_tiles, 0), 0)
    row_spec = lambda ix: pl.BlockSpec((tm, D_MODEL), ix)
    chunk = lambda i: jnp.minimum(i, N_FC - 1)

    in_specs = [row_spec(prompt_ix), row_spec(sample_ix)] if first else [row_spec(lambda i: (i, 0))]
    in_specs += [
        pl.BlockSpec((None, None, 1, D_MODEL), const(l, nidx, 0, 0)),
        pl.BlockSpec((None, None, 1, D_MODEL), const(l, nidx, 0, 0)),
        pl.BlockSpec((D_MODEL, D_FF), const(0, 0), pipeline_mode=pl.Buffered(1)),
        pl.BlockSpec((D_MODEL, D_FF), const(0, 0), pipeline_mode=pl.Buffered(1)),
        pl.BlockSpec((D_FF, D_MODEL), const(0, 0), pipeline_mode=pl.Buffered(1)),
    ]
    args = list(xs) + [gpre4, gpost4] + list(w_bf16)
    if convert:
        nl, ni = nxt_l_idx
        in_specs += [
            pl.BlockSpec((None, None, D_MODEL, FFN_FC), lambda i: (nl, ni, 0, chunk(i))),
            pl.BlockSpec((None, None, D_MODEL, FFN_FC), lambda i: (nl, ni, 0, chunk(i))),
            pl.BlockSpec((None, None, FFN_FC, D_MODEL), lambda i: (nl, ni, chunk(i), 0)),
        ]
        args += list(w_f32_next)
    in_chunk = lambda i: jnp.minimum(i, MIX_IN_CHUNKS - 1)
    out_chunk = lambda i: jnp.minimum(i, MIX_OUT_CHUNKS - 1)
    if mix:
        in_specs += [
            pl.BlockSpec((None, MIX_IN_ROWS, D_MODEL), lambda i: (l, in_chunk(i), 0)),
            pl.BlockSpec((None, MIX_OUT_ROWS, D_MODEL), lambda i: (l, out_chunk(i), 0)),
        ]
        args += list(mix_f32)

    if last:
        out_specs = [row_spec(prompt_ix), row_spec(sample_ix)]
        out_shape = [jax.ShapeDtypeStruct((N_PROMPT, D_MODEL), F32),
                     jax.ShapeDtypeStruct((N_SAMPLE, D_MODEL), F32)]
    else:
        out_specs = [row_spec(lambda i: (i, 0))]
        out_shape = [jax.ShapeDtypeStruct((N_ROWS, D_MODEL), F32)]
    n_act = len(out_specs)
    if convert:
        out_specs += [
            pl.BlockSpec((D_MODEL, FFN_FC), lambda i: (0, chunk(i))),
            pl.BlockSpec((D_MODEL, FFN_FC), lambda i: (0, chunk(i))),
            pl.BlockSpec((FFN_FC, D_MODEL), lambda i: (chunk(i), 0)),
        ]
        out_shape += [jax.ShapeDtypeStruct((D_MODEL, D_FF), BF16),
                      jax.ShapeDtypeStruct((D_MODEL, D_FF), BF16),
                      jax.ShapeDtypeStruct((D_FF, D_MODEL), BF16)]
    n_cvt = len(out_specs)
    if mix:
        out_specs += [
            pl.BlockSpec((MIX_IN_ROWS, D_MODEL), lambda i: (in_chunk(i), 0)),
            pl.BlockSpec((MIX_OUT_ROWS, D_MODEL), lambda i: (out_chunk(i), 0)),
        ]
        out_shape += [jax.ShapeDtypeStruct((D_IN, D_MODEL), BF16),
                      jax.ShapeDtypeStruct((D_MODEL, D_MODEL), BF16)]

    res = pl.pallas_call(
        functools.partial(_ffn_kernel, first=first, last=last, convert=convert, mix=mix,
                          np_tiles=np_tiles),
        grid=(N_ROWS // tm,),
        in_specs=in_specs,
        out_specs=out_specs,
        out_shape=out_shape,
        scratch_shapes=[pltpu.VMEM((tm, D_MODEL), F32)] if first else [],
        input_output_aliases={} if (first or last) else {0: 0},
        compiler_params=pltpu.CompilerParams(
            dimension_semantics=("arbitrary",), vmem_limit_bytes=VMEM_LIMIT),
        name=f"ffn_{sub}",
    )(*args)
    return res[:n_act], tuple(res[n_act:n_cvt]), tuple(res[n_cvt:])


def _grouped(x):
    return x.reshape(x.shape[0] // SUBLANES, SUBLANES, x.shape[1])


def _sub_iota(x3):
    return lax.broadcasted_iota(jnp.int32, x3.shape, 1)


def _groups_down(x3, q, first=None):
    if q == 0:
        return x3
    head = jnp.zeros((q,) + x3.shape[1:], x3.dtype) if first is None else first
    return jnp.concatenate([head, x3[:x3.shape[0] - q]], axis=0)


def _groups_up(x3, q):
    if q == 0:
        return x3
    return jnp.concatenate([x3[q:], jnp.zeros((q,) + x3.shape[1:], x3.dtype)], axis=0)


def _shift_down(x, k, first=None, in_group=False):
    x3 = _grouped(x)
    q, r = divmod(k, SUBLANES)
    if r == 0:
        y = _groups_down(x3, q)
    else:
        xr = pltpu.roll(x3, r, 1)
        if in_group:
            y = xr
        else:
            fr = None if first is None else pltpu.roll(_grouped(first), r, 1)
            y = jnp.where(_sub_iota(x3) < r, _groups_down(xr, q + 1, fr if q == 0 else None),
                          _groups_down(xr, q))
    return y.reshape(x.shape)


def _shift_up(x, k, in_group=False):
    x3 = _grouped(x)
    q, r = divmod(k, SUBLANES)
    if r == 0:
        y = _groups_up(x3, q)
    else:
        xr = pltpu.roll(x3, SUBLANES - r, 1)
        if in_group:
            y = xr
        else:
            y = jnp.where(_sub_iota(x3) >= SUBLANES - r, _groups_up(xr, q + 1), _groups_up(xr, q))
    return y.reshape(x.shape)


def _row_in_seg(shape, seg):
    return lax.broadcasted_iota(jnp.int32, shape, 0) & (seg - 1)


def _scan_affine8(a, b):
    t = _row_in_seg(a.shape, SUBLANES)
    d = 1
    while d < SUBLANES:
        m = t >= d
        b = jnp.where(m, a * _shift_down(b, d, in_group=True) + b, b)
        a = jnp.where(m, a * _shift_down(a, d, in_group=True), a)
        d *= 2
    return a, b


def _seg_cumsum(x, seg):
    t = _row_in_seg(x.shape, seg)
    d = 1
    while d < seg:
        x = jnp.where(t >= d, x + _shift_down(x, d, in_group=(seg == SUBLANES)), x)
        d *= 2
    return x


def _causal_conv(xl, convw_ref, bias, shifted):
    xc = None
    for jj in range(CONV_W):
        kk = CONV_W - 1 - jj
        term = (xl if kk == 0 else shifted(kk)) * convw_ref[jj:jj + 1, :]
        xc = term if xc is None else xc + term
    return xc + bias


def _lru_gates(xc, vec_ref, wa_ref, wx_ref):
    half = D_LRU // 2
    xb = xc.astype(BF16)
    xb0, xb1 = xb[:, :half], xb[:, half:]
    ba, bx, lam = vec_ref[1:2, :], vec_ref[2:3, :], vec_ref[3:4, :]
    r = jax.nn.sigmoid(
        jnp.concatenate([_dot(xb0, wa_ref[0]), _dot(xb1, wa_ref[1])], axis=1) + ba)
    i = jax.nn.sigmoid(
        jnp.concatenate([_dot(xb0, wx_ref[0]), _dot(xb1, wx_ref[1])], axis=1) + bx)
    log_a = -LRU_C * r * _softplus(-lam)
    a = jnp.exp(log_a)
    m2 = -jnp.tanh(log_a) * (a * a + 1.0)
    mult = jnp.where(m2 > 0.0, m2 * lax.rsqrt(m2), 0.0)
    return a, mult, i


def _lru_carry(a8, h8, hin):
    hs = []
    for gidx in range(a8.shape[0] // SUBLANES):
        rows = slice(SUBLANES * gidx, SUBLANES * (gidx + 1))
        hg = a8[rows] * hin + h8[rows]
        hin = hg[SUBLANES - 1:SUBLANES, :]
        hs.append(hg)
    return hs


def _level_masks(nlev):
    t = lax.broadcasted_iota(jnp.int32, (1, BLK, BLK), 1)
    s = lax.broadcasted_iota(jnp.int32, (1, BLK, BLK), 2)
    masks = [t == s]
    for lv in range(nlev):
        masks.append((((t >> lv) ^ (s >> lv)) == 1) & (((t >> lv) & 1) == 1))
    return masks


def _head_halves(val, dtype):
    second = (lax.broadcasted_iota(jnp.int32, val.shape, 1) & (LANES - 1)) >= GLA_DK
    zero = jnp.zeros_like(val)
    return (jnp.where(second, zero, val).astype(dtype), jnp.where(second, val, zero).astype(dtype))


def _gla_operands(q, k, gk, seg, state_dtype):
    t = lax.broadcasted_iota(jnp.int32, gk.shape, 0)
    b = _seg_cumsum(gk, seg)
    qs = [_head_halves(q, BF16)]
    ks = [k.astype(BF16)]
    ref = b - gk
    end = b
    s = 1
    while s < seg:
        qs.append(_head_halves(q * jnp.exp(b - ref), BF16))
        ks.append((k * jnp.exp(end - b)).astype(BF16))
        bit = (t & s) != 0
        ref = jnp.where(bit, _shift_down(ref, s, in_group=True), ref)
        end = jnp.where(bit, end, _shift_up(end, s, in_group=True))
        s *= 2
    qe = _head_halves(q * jnp.exp(b), state_dtype)
    ke = _head_halves(k * jnp.exp(end - b), state_dtype)
    return qs, ks, qe, ke, end


def _gla_operands_bounded(q, k, gk):
    n, c = gk.shape
    b = _seg_cumsum(gk, GLA_CHUNK)
    b3 = b.reshape(n // GLA_CHUNK, GLA_CHUNK, c)
    total = jnp.broadcast_to(b3[:, GLA_CHUNK - 1:GLA_CHUNK, :], b3.shape).reshape(n, c)
    qe = _head_halves(q * jnp.exp(b), BF16)
    ke = _head_halves(k * jnp.exp(total - b), BF16)
    kf = (k * jnp.exp(-b)).astype(BF16)
    return qe, ke, kf, total


def _chunk_decay_floor(gk):
    n, c = gk.shape
    return jnp.min(jnp.sum(gk.reshape(n // GLA_CHUNK, GLA_CHUNK, c), axis=1))


def _blocks(x, lanes):
    return x[:, lanes].reshape(x.shape[0] // BLK, BLK, LANES)


def _scores_bounded(qe, kf, h):
    grp, half = divmod(h, 2)
    lanes = slice(LANES * grp, LANES * (grp + 1))
    t = lax.broadcasted_iota(jnp.int32, (1, BLK, BLK), 1)
    s = lax.broadcasted_iota(jnp.int32, (1, BLK, BLK), 2)
    sc = _bdot('ctd,csd->cts', _blocks(qe[half], lanes), _blocks(kf, lanes))
    return jnp.where(t >= s, sc, 0.0).astype(BF16)


def _scores(qs, ks, masks, h):
    grp, half = divmod(h, 2)
    lanes = slice(LANES * grp, LANES * (grp + 1))
    p = None
    for lv, m in enumerate(masks):
        sc = _bdot('ctd,csd->cts', _blocks(qs[lv][half], lanes), _blocks(ks[lv], lanes))
        p = jnp.where(m, sc, 0.0 if p is None else p)
    return p.astype(BF16)


def _decay_rows(bl_row):
    return jnp.exp(jnp.transpose(jnp.broadcast_to(bl_row, (LANES, LANES))))


def _gla_finish(o, g, gn):
    outs = []
    for h in range(GLA_HEADS):
        sl = slice(GLA_DV * h, GLA_DV * (h + 1))
        outs.append(_rms(o[:, sl], gn) * jax.nn.silu(g[:, sl]))
    return jnp.concatenate(outs, axis=1)


def _gk_from(gd, wgate_ref, bgate_ref):
    z = _dot(gd.astype(BF16), wgate_ref[...]) + bgate_ref[...]
    return -_softplus(-z) * (1.0 / GLA_NORMALIZER)


def _out_proj(x, lru_out, gla_out, w_out_ref, gpost_ref):
    m = (_dot(lru_out.astype(BF16), w_out_ref[:D_LRU, :])
         + _dot(gla_out.astype(BF16), w_out_ref[D_LRU:, :]))
    return x + _rms(m, gpost_ref[...])


def _prompt_mixer_kernel(
        x_ref, gpre_ref, gpost_ref, w_lru_ref, w_qk_ref, w_v_ref, w_g_ref, w_gd_ref,
        convw_ref, vec_ref, wa_ref, wx_ref, wgate_ref, bgate_ref, gn_ref, w_out_ref,
        o_ref, hlast_ref, ctail_ref, sfin_ref,
        tail_sc, h_sc, s_sc, x_sc, xy_sc, qk_sc, gk_sc, v_sc, g_sc, bounded_sc):
    s = pl.program_id(0)
    tt = PROMPT_TT
    nt = SEQ // tt
    nchunk = tt // BLK
    seq_start = lax.rem(s + nt - 1, nt) == 0

    @pl.when(s == 0)
    def _():
        for ref in (x_sc, xy_sc, qk_sc, gk_sc, v_sc, g_sc):
            ref[...] = jnp.zeros_like(ref)
        bounded_sc[0] = 1

    @pl.when((s == 0) | seq_start)
    def _():
        tail_sc[...] = jnp.zeros_like(tail_sc)
        h_sc[...] = jnp.zeros_like(h_sc)
        s_sc[...] = jnp.zeros_like(s_sc)

    def body(bounded):
        x_new = x_ref[...]
        hn = _rms(x_new, gpre_ref[...]).astype(BF16)

        xy = xy_sc[...]
        xl, yl = xy[:, :D_LRU], xy[:, D_LRU:]
        tail = tail_sc[...]
        xc = _causal_conv(xl, convw_ref, vec_ref[0:1, :],
                          lambda kk: _shift_down(xl, kk, first=tail))
        tail_sc[...] = xl[tt - SUBLANES:, :]
        ctail_ref[...] = xl[tt - SUBLANES:, :]
        xy_sc[...] = _dot_nt(hn, w_lru_ref[...])
        a, mult, gi = _lru_gates(xc, vec_ref, wa_ref, wx_ref)

        qk = qk_sc[...]
        q, k = qk[:, :D_QK] * (GLA_DK ** -0.5), qk[:, D_QK:]
        gk = gk_sc[...]
        if bounded:
            qe, ke, kf, bl = _gla_operands_bounded(q, k, gk)
        else:
            qs, ks, qe, ke, bl = _gla_operands(q, k, gk, GLA_CHUNK, BF16)
            masks = _level_masks(6)
        qk_sc[...] = _dot_nt(hn, w_qk_ref[...])
        gk_new = _gk_from(_dot_nt(hn, w_gd_ref[...]), wgate_ref, bgate_ref)
        gk_sc[...] = gk_new
        bounded_sc[0] = (_chunk_decay_floor(gk_new) >= -GLA_FAST_LIMIT).astype(jnp.int32)

        first_row = (lax.broadcasted_iota(jnp.int32, a.shape, 0) == 0) & seq_start
        mult_r = jnp.where(first_row, 1.0, mult)
        a8, h8 = _scan_affine8(a, mult_r * gi * xc)
        hs = _lru_carry(a8, h8, h_sc[SUBLANES - 1:SUBLANES, :])
        h_sc[...] = hs[-1]
        hlast_ref[...] = hs[-1]
        lru_out = _rms(jnp.concatenate(hs, axis=0) * jax.nn.gelu(yl), vec_ref[4:5, :])

        vb = v_sc[...]
        ps, vhs, upds = [], [], []
        for h in range(GLA_HEADS):
            grp, half = divmod(h, 2)
            lanes = slice(LANES * grp, LANES * (grp + 1))
            ps.append(_scores_bounded(qe, kf, h) if bounded else _scores(qs, ks, masks, h))
            vhs.append(vb[:, GLA_DV * h:GLA_DV * (h + 1)].reshape(nchunk, BLK, GLA_DV))
            upds.append(_bdot('csd,csv->cdv', _blocks(ke[half], lanes), vhs[h]))
        v_sc[...] = _dot_nt(hn, w_v_ref[...]).astype(BF16)
        g_new = _dot_nt(hn, w_g_ref[...])

        decs = [[_decay_rows(bl[c * BLK:c * BLK + 1, LANES * grp:LANES * (grp + 1)])
                 for c in range(nchunk)] for grp in range(GLA_HEADS // 2)]
        outs = []
        for h in range(GLA_HEADS):
            grp, half = divmod(h, 2)
            lanes = slice(LANES * grp, LANES * (grp + 1))
            st = s_sc[h]
            states = []
            for c in range(nchunk):
                states.append(st.astype(BF16))
                st = decs[grp][c] * st + upds[h][c]
            s_sc[h] = st
            sfin_ref[h] = st[GLA_DK * half:GLA_DK * (half + 1), :]
            o = (_bdot('cts,csv->ctv', ps[h], vhs[h])
                 + _bdot('ctd,cdv->ctv', _blocks(qe[half], lanes), jnp.stack(states)))
            outs.append(o.reshape(tt, GLA_DV))
        gla_out = _gla_finish(jnp.concatenate(outs, axis=1), g_sc[...], gn_ref[...])
        g_sc[...] = g_new

        o_ref[...] = _out_proj(x_sc[...], lru_out, gla_out, w_out_ref, gpost_ref)
        x_sc[...] = x_new

    lax.cond(bounded_sc[0] != 0, functools.partial(body, True), functools.partial(body, False))


def _weight_specs(l, const):
    d = D_MODEL
    return [
        pl.BlockSpec((None, None, 1, d), const(l, 1, 0, 0)),
        pl.BlockSpec((None, None, 1, d), const(l, 1, 0, 0)),
        pl.BlockSpec((2 * D_LRU, d), const(0, 0)),
        pl.BlockSpec((2 * D_QK, d), const(2, 0)),
        pl.BlockSpec((D_GLA, d), const(3, 0)),
        pl.BlockSpec((D_GLA, d), const(4, 0)),
        pl.BlockSpec((GLA_RANK, d), const((D_IN - GLA_RANK) // GLA_RANK, 0)),
        pl.BlockSpec((None, CONV_W, D_LRU), const(l, 0, 0)),
        pl.BlockSpec((None, 5, D_LRU), const(l, 0, 0)),
        pl.BlockSpec((None, 2, MXU_DIM, MXU_DIM), const(l, 0, 0, 0)),
        pl.BlockSpec((None, 2, MXU_DIM, MXU_DIM), const(l, 0, 0, 0)),
        pl.BlockSpec((None, GLA_RANK, D_QK), const(l, 0, 0)),
        pl.BlockSpec((None, 1, D_QK), const(l, 0, 0)),
        pl.BlockSpec((None, 1, GLA_DV), const(l, 0, 0)),
        pl.BlockSpec((d, d), const(0, 0)),
    ]


def _prompt_mixer_call(x_all, weights, l):
    tt = PROMPT_TT
    nt = SEQ // tt
    ntiles = BATCH * nt
    const = lambda *ix: (lambda s: ix)
    done = lambda s: jnp.maximum(s - 1, 0)
    out_shapes = (
        jax.ShapeDtypeStruct((N_ROWS, D_MODEL), F32),
        jax.ShapeDtypeStruct((BATCH, SUBLANES, D_LRU), F32),
        jax.ShapeDtypeStruct((BATCH, SUBLANES, D_LRU), F32),
        jax.ShapeDtypeStruct((BATCH, GLA_HEADS, GLA_DK, GLA_DV), F32),
    )
    out_specs = (
        pl.BlockSpec((tt, D_MODEL), lambda s: (done(s), 0)),
        pl.BlockSpec((None, SUBLANES, D_LRU), lambda s: (done(s) // nt, 0, 0)),
        pl.BlockSpec((None, SUBLANES, D_LRU), lambda s: (done(s) // nt, 0, 0)),
        pl.BlockSpec((None, GLA_HEADS, GLA_DK, GLA_DV), lambda s: (done(s) // nt, 0, 0, 0)),
    )
    scratch = [
        pltpu.VMEM((SUBLANES, D_LRU), F32),
        pltpu.VMEM((SUBLANES, D_LRU), F32),
        pltpu.VMEM((GLA_HEADS, LANES, GLA_DV), F32),
        pltpu.VMEM((tt, D_MODEL), F32),
        pltpu.VMEM((tt, 2 * D_LRU), F32),
        pltpu.VMEM((tt, 2 * D_QK), F32),
        pltpu.VMEM((tt, D_QK), F32),
        pltpu.VMEM((tt, D_GLA), BF16),
        pltpu.VMEM((tt, D_GLA), F32),
        pltpu.SMEM((1,), jnp.int32),
    ]
    return pl.pallas_call(
        _prompt_mixer_kernel,
        grid=(ntiles + 1,),
        in_specs=[pl.BlockSpec((tt, D_MODEL), lambda s: (jnp.minimum(s, ntiles - 1), 0))]
                 + _weight_specs(l, const),
        out_specs=out_specs,
        out_shape=out_shapes,
        scratch_shapes=scratch,
        input_output_aliases={0: 0},
        compiler_params=pltpu.CompilerParams(
            dimension_semantics=("arbitrary",), vmem_limit_bytes=VMEM_LIMIT),
        name="mixer_prompt",
    )(x_all, *weights)


def _sample_mixer_kernel(
        x_ref, gpre_ref, gpost_ref, w_lru_ref, w_qk_ref, w_v_ref, w_g_ref, w_gd_ref,
        convw_ref, vec_ref, wa_ref, wx_ref, wgate_ref, bgate_ref, gn_ref, w_out_ref,
        h0_ref, tail_ref, s0_ref, *rest, carried):
    o_ref, hseq_ref, xl_ref, snew_ref = rest[1:] if carried else rest
    rows = SAMPLE_BT * DEC_SEQ
    nseq = SAMPLE_BT
    x = x_ref[...]
    hn = _rms(x, gpre_ref[...]).astype(BF16)
    xy = _dot_nt(hn, w_lru_ref[...])
    xl, yl = xy[:, :D_LRU], xy[:, D_LRU:]
    qk = _dot_nt(hn, w_qk_ref[...])
    q, k = qk[:, :D_QK] * (GLA_DK ** -0.5), qk[:, D_QK:]
    v = _dot_nt(hn, w_v_ref[...])
    g = _dot_nt(hn, w_g_ref[...])
    gd = _dot_nt(hn, w_gd_ref[...])

    t8 = _row_in_seg(xl.shape, DEC_SEQ)
    tail = tail_ref[...]
    xc = _causal_conv(
        xl, convw_ref, vec_ref[0:1, :],
        lambda kk: jnp.where(t8 < kk, _shift_down(tail, kk, in_group=True),
                             _shift_down(xl, kk, in_group=True)))
    xl_ref[...] = xl

    a, mult, gi = _lru_gates(xc, vec_ref, wa_ref, wx_ref)
    bterm = mult * gi * xc + a * h0_ref[...]
    _, hseq = _scan_affine8(a, bterm)
    hseq_ref[...] = hseq
    lru_out = _rms(hseq * jax.nn.gelu(yl), vec_ref[4:5, :])

    gk = _gk_from(gd, wgate_ref, bgate_ref)
    qs, ks, qe, ke, bl = _gla_operands(q, k, gk, DEC_SEQ, F32)
    masks = _level_masks(3)
    vb = v.astype(BF16)
    zeros_half = jnp.zeros((nseq, GLA_DK, GLA_DV), F32)

    def per_seq(x2):
        return x2.reshape(nseq, DEC_SEQ, x2.shape[1]).astype(BF16)

    decs = []
    for grp in range(GLA_HEADS // 2):
        bl3 = bl[:, LANES * grp:LANES * (grp + 1)].reshape(nseq, DEC_SEQ, LANES)
        decs.append([_decay_rows(bl3[b, 0:1, :]) for b in range(nseq)])
    outs = []
    for h in range(GLA_HEADS):
        grp, half = divmod(h, 2)
        lanes = slice(LANES * grp, LANES * (grp + 1))
        cols = slice(GLA_DV * h, GLA_DV * (h + 1))
        ks_rows = slice(GLA_DK * half, GLA_DK * (half + 1))
        p = _scores(qs, ks, masks, h)
        vh = vb[:, cols].reshape(rows // BLK, BLK, GLA_DV)
        o_intra = _bdot('cts,csv->ctv', p, vh).reshape(rows, GLA_DV)
        s0 = s0_ref[:, h]
        s_pad = jnp.concatenate([zeros_half, s0] if half else [s0, zeros_half], axis=1)
        o_inter = _bdot('btd,bdv->btv', per_seq(qe[half][:, lanes]), s_pad.astype(BF16))
        outs.append(o_intra + o_inter.reshape(rows, GLA_DV))
        upd = _bdot('bsd,bsv->bdv', per_seq(ke[half][:, lanes]), per_seq(v[:, cols]))
        for b in range(nseq):
            snew_ref[b, h] = decs[grp][b][ks_rows] * s0[b] + upd[b, ks_rows, :]
    gla_out = _gla_finish(jnp.concatenate(outs, axis=1), g, gn_ref[...])

    o_ref[...] = _out_proj(x, lru_out, gla_out, w_out_ref, gpost_ref)


def _sample_mixer_call(x_all, weights, h0pad, tailpad, s0, snew_all, l):
    rows = SAMPLE_BT * DEC_SEQ
    nb = DEC_BATCH // SAMPLE_BT
    base = N_PROMPT // rows
    const = lambda *ix: (lambda i: ix)
    st_spec = pl.BlockSpec((None, SAMPLE_BT, GLA_HEADS, GLA_DK, GLA_DV),
                           lambda i: (l, i, 0, 0, 0))
    out_shapes = (
        jax.ShapeDtypeStruct((N_ROWS, D_MODEL), F32),
        jax.ShapeDtypeStruct((N_SAMPLE, D_LRU), F32),
        jax.ShapeDtypeStruct((N_SAMPLE, D_LRU), F32),
        jax.ShapeDtypeStruct((DEPTH, DEC_BATCH, GLA_HEADS, GLA_DK, GLA_DV), F32),
    )
    out_specs = (
        pl.BlockSpec((rows, D_MODEL), lambda i: (base + i, 0)),
        pl.BlockSpec((rows, D_LRU), lambda i: (i, 0)),
        pl.BlockSpec((rows, D_LRU), lambda i: (i, 0)),
        st_spec,
    )
    in_specs = ([pl.BlockSpec((rows, D_MODEL), lambda i: (base + i, 0))]
                + _weight_specs(l, const)
                + [pl.BlockSpec((None, rows, D_LRU), lambda i: (l, i, 0)),
                   pl.BlockSpec((None, rows, D_LRU), lambda i: (l, i, 0)),
                   st_spec])
    args = [x_all, *weights, h0pad, tailpad, s0]
    aliases = {0: 0}
    if snew_all is not None:
        aliases[len(args)] = 3
        in_specs.append(pl.BlockSpec(memory_space=pl.ANY))
        args.append(snew_all)
    return pl.pallas_call(
        functools.partial(_sample_mixer_kernel, carried=snew_all is not None),
        grid=(nb,),
        in_specs=in_specs,
        out_specs=out_specs,
        out_shape=out_shapes,
        input_output_aliases=aliases,
        compiler_params=pltpu.CompilerParams(
            dimension_semantics=("parallel",), vmem_limit_bytes=VMEM_LIMIT),
        name="mixer_sample",
    )(*args)


def _blockdiag(w):
    per = MXU_DIM // LRU_BLOCK
    w = w.reshape(DEPTH, N_LRU_BLOCKS // per, per, LRU_BLOCK, LRU_BLOCK)
    eye = jnp.eye(per, dtype=w.dtype)
    bd = jnp.einsum('lpiab,ij->lpiajb', w, eye)
    return bd.reshape(DEPTH, N_LRU_BLOCKS // per, MXU_DIM, MXU_DIM)


def kernel(x_prompt, x_sample, state_lru_h, state_lru_conv, state_gla, norm_pre, norm_post, w_ffn_gate, w_ffn_up, w_ffn_down, w_in, conv_w, conv_b, lru_wa, lru_ba, lru_wx, lru_bx, lru_lambda, lru_norm, gla_w_gate, gla_b_gate, gla_norm, w_out):
    gpre4 = norm_pre.reshape(DEPTH, 3, 1, D_MODEL)
    gpost4 = norm_post.reshape(DEPTH, 3, 1, D_MODEL)
    w_ffn_f32 = (w_ffn_gate, w_ffn_up, w_ffn_down)
    vec = jnp.stack([conv_b, lru_ba, lru_bx, lru_lambda, lru_norm], axis=1)
    w_in_t = jnp.swapaxes(w_in, 1, 2)
    small = (conv_w, vec,
             _blockdiag(lru_wa).astype(BF16), _blockdiag(lru_wx).astype(BF16),
             gla_w_gate.astype(BF16), gla_b_gate.reshape(DEPTH, 1, D_QK),
             gla_norm.reshape(DEPTH, 1, GLA_DV))

    def mixer_weights(w_in_l, w_out_l):
        return (gpre4, gpost4) + (w_in_l,) * 5 + small + (w_out_l,)
    h0pad = jnp.pad(state_lru_h[:, :, None, :],
                    ((0, 0), (0, 0), (0, DEC_SEQ - 1), (0, 0))).reshape(DEPTH, N_SAMPLE, D_LRU)
    tailpad = jnp.pad(state_lru_conv,
                      ((0, 0), (0, 0), (SUBLANES - (CONV_W - 1), 0), (0, 0))
                      ).reshape(DEPTH, N_SAMPLE, D_LRU)

    w_bf16 = tuple(w[0, 0].astype(BF16) for w in w_ffn_f32)
    ffn_order = [(l, idx) for l in range(DEPTH) for idx in range(2)]

    def ffn(xs, l, idx, first=False, last=False):
        pos = ffn_order.index((l, idx))
        nxt = ffn_order[pos + 1] if pos + 1 < len(ffn_order) else None
        return _ffn_call(xs, gpre4, gpost4, w_bf16, w_ffn_f32 if nxt else None,
                         (w_in_t, w_out) if idx == 0 else None, l, idx, nxt, "ab"[idx], first,
                         last)

    hp, cp, sp, hs, cs = [], [], [], [], []
    snew_all = None
    xs = (x_prompt.reshape(N_PROMPT, D_MODEL), x_sample.reshape(N_SAMPLE, D_MODEL))
    for l in range(DEPTH):
        (x_all,), w_bf16, mix_bf16 = ffn(xs, l, 0, first=(l == 0))
        weights = mixer_weights(*mix_bf16)
        x_all, hl, ct, sf = _prompt_mixer_call(x_all, weights, l)
        hp.append(hl[:, SUBLANES - 1])
        cp.append(ct[:, SUBLANES - (CONV_W - 1):])
        sp.append(sf)
        x_all, hseq, xl, snew_all = _sample_mixer_call(
            x_all, weights, h0pad, tailpad, state_gla, snew_all, l)
        hs.append(hseq.reshape(DEC_BATCH, DEC_SEQ, D_LRU)[:, DEC_SEQ - 1])
        cs.append(xl.reshape(DEC_BATCH, DEC_SEQ, D_LRU)[:, DEC_SEQ - (CONV_W - 1):])
        xs, w_bf16, _ = ffn((x_all,), l, 1, last=(l == DEPTH - 1))

    yp = xs[0].reshape(BATCH, SEQ, D_MODEL)
    ys = xs[1].reshape(DEC_BATCH, DEC_SEQ, D_MODEL)
    return (yp, ys, jnp.stack(hp), jnp.stack(cp), jnp.stack(sp),
            jnp.stack(hs), jnp.stack(cs), snew_all)
```

```python
import functools

import jax
import jax.numpy as jnp
from jax import lax
from jax.experimental import pallas as pl
from jax.experimental.pallas import tpu as pltpu

D_MODEL = 1024
BATCH = 8
SEQ = 2048
DEPTH = 4
DEC_BATCH = 128
DEC_SEQ = 8
D_LRU = 512
N_LRU_BLOCKS = 8
LRU_BLOCK = 64
CONV_W = 4
LRU_C = 8.0
D_GLA = 512
GLA_HEADS = 4
GLA_DV = 128
GLA_DK = 64
GLA_RANK = 16
GLA_NORMALIZER = 16.0
GLA_CHUNK = 64
D_FF = 2816
EPS = 1e-6
D_QK = GLA_HEADS * GLA_DK
D_IN = 2 * D_LRU + 2 * D_QK + 2 * D_GLA + GLA_RANK

N_PROMPT = BATCH * SEQ
N_SAMPLE = DEC_BATCH * DEC_SEQ
N_ROWS = N_PROMPT + N_SAMPLE

SUBLANES = 8
LANES = 128
MXU_DIM = 256
VMEM_LIMIT = 56 * 1024 * 1024

FFN_TM = 1024
FFN_TM_FIRST = 512
FFN_FC = 256
FFN_MM = 256
PROMPT_TT = 512
SAMPLE_BT = 32
BLK = 64
GLA_FAST_LIMIT = 40.0

F32 = jnp.float32
BF16 = jnp.bfloat16


def _rms(x, g):
    ms = jnp.mean(x * x, axis=-1, keepdims=True)
    return x * lax.rsqrt(ms + EPS) * g


def _softplus(x):
    return jnp.maximum(x, 0.0) + jnp.log(1.0 + jnp.exp(-jnp.abs(x)))


def _dot(a, b):
    return jnp.dot(a, b, preferred_element_type=F32)


def _dot_nt(a, b):
    return lax.dot_general(a, b, (((1,), (1,)), ((), ())), preferred_element_type=F32)


def _bdot(spec, a, b):
    return jnp.einsum(spec, a, b, preferred_element_type=F32)


N_FC = D_FF // FFN_FC
MIX_IN_CHUNKS = 7
MIX_IN_ROWS = D_IN // MIX_IN_CHUNKS
MIX_OUT_CHUNKS = 8
MIX_OUT_ROWS = D_MODEL // MIX_OUT_CHUNKS


def _ffn_kernel(*refs, first, last, convert, mix, np_tiles):
    it = iter(refs)
    x_refs = (next(it), next(it)) if first else (next(it),)
    gpre_ref, gpost_ref, wg_ref, wu_ref, wd_ref = (next(it) for _ in range(5))
    nxt = tuple(next(it) for _ in range(3)) if convert else ()
    mix_in = tuple(next(it) for _ in range(2)) if mix else ()
    o_refs = (next(it), next(it)) if last else (next(it),)
    cvt = tuple(next(it) for _ in range(3)) if convert else ()
    mix_out = tuple(next(it) for _ in range(2)) if mix else ()
    i = pl.program_id(0)

    if first:
        x_sc = next(it)

        @pl.when(i < np_tiles)
        def _():
            x_sc[...] = x_refs[0][...]

        @pl.when(i >= np_tiles)
        def _():
            x_sc[...] = x_refs[1][...]

        x = x_sc[...]
    else:
        x = x_refs[0][...]

    h = _rms(x, gpre_ref[...]).astype(BF16)
    bounds = list(range(0, D_FF, FFN_MM)) + [D_FF]
    spans = list(zip(bounds[:-1], bounds[1:]))

    def gate_up(span):
        return _dot(h, wg_ref[:, span[0]:span[1]]), _dot(h, wu_ref[:, span[0]:span[1]])

    acc = jnp.zeros((x.shape[0], D_MODEL), F32)
    g, u = gate_up(spans[0])
    for n, span in enumerate(spans):
        g_next, u_next = gate_up(spans[n + 1]) if n + 1 < len(spans) else (None, None)
        act = (jax.nn.silu(g) * u).astype(BF16)
        acc = acc + _dot(act, wd_ref[span[0]:span[1], :])
        g, u = g_next, u_next
    out = x + _rms(acc, 0.5 * gpost_ref[...])

    if last:
        @pl.when(i < np_tiles)
        def _():
            o_refs[0][...] = out

        @pl.when(i >= np_tiles)
        def _():
            o_refs[1][...] = out
    else:
        o_refs[0][...] = out

    if convert:
        @pl.when(i < N_FC)
        def _():
            for src, dst in zip(nxt, cvt):
                dst[...] = src[...].astype(BF16)

    if mix:
        @pl.when(i < MIX_IN_CHUNKS)
        def _():
            mix_out[0][...] = mix_in[0][...].astype(BF16)

        @pl.when(i < MIX_OUT_CHUNKS)
        def _():
            mix_out[1][...] = mix_in[1][...].astype(BF16)


def _ffn_call(xs, gpre4, gpost4, w_bf16, w_f32_next, mix_f32, l, idx, nxt_l_idx, sub, first,
              last):
    convert = w_f32_next is not None
    mix = mix_f32 is not None
    const = lambda *ix: (lambda i: ix)
    nidx = 2 * idx
    tm = FFN_TM_FIRST if first else FFN_TM
    np_tiles = N_PROMPT // tm
    prompt_ix = lambda i: (jnp.minimum(i, np_tiles - 1), 0)
    sample_ix = lambda i: (jnp.maximum(i - np_tiles, 0), 0)
    row_spec = lambda ix: pl.BlockSpec((tm, D_MODEL), ix)
    chunk = lambda i: jnp.minimum(i, N_FC - 1)

    in_specs = [row_spec(prompt_ix), row_spec(sample_ix)] if first else [row_spec(lambda i: (i, 0))]
    in_specs += [
        pl.BlockSpec((None, None, 1, D_MODEL), const(l, nidx, 0, 0)),
        pl.BlockSpec((None, None, 1, D_MODEL), const(l, nidx, 0, 0)),
        pl.BlockSpec((D_MODEL, D_FF), const(0, 0), pipeline_mode=pl.Buffered(1)),
        pl.BlockSpec((D_MODEL, D_FF), const(0, 0), pipeline_mode=pl.Buffered(1)),
        pl.BlockSpec((D_FF, D_MODEL), const(0, 0), pipeline_mode=pl.Buffered(1)),
    ]
    args = list(xs) + [gpre4, gpost4] + list(w_bf16)
    if convert:
        nl, ni = nxt_l_idx
        in_specs += [
            pl.BlockSpec((None, None, D_MODEL, FFN_FC), lambda i: (nl, ni, 0, chunk(i))),
            pl.BlockSpec((None, None, D_MODEL, FFN_FC), lambda i: (nl, ni, 0, chunk(i))),
            pl.BlockSpec((None, None, FFN_FC, D_MODEL), lambda i: (nl, ni, chunk(i), 0)),
        ]
        args += list(w_f32_next)
    in_chunk = lambda i: jnp.minimum(i, MIX_IN_CHUNKS - 1)
    out_chunk = lambda i: jnp.minimum(i, MIX_OUT_CHUNKS - 1)
    if mix:
        in_specs += [
            pl.BlockSpec((None, MIX_IN_ROWS, D_MODEL), lambda i: (l, in_chunk(i), 0)),
            pl.BlockSpec((None, MIX_OUT_ROWS, D_MODEL), lambda i: (l, out_chunk(i), 0)),
        ]
        args += list(mix_f32)

    if last:
        out_specs = [row_spec(prompt_ix), row_spec(sample_ix)]
        out_shape = [jax.ShapeDtypeStruct((N_PROMPT, D_MODEL), F32),
                     jax.ShapeDtypeStruct((N_SAMPLE, D_MODEL), F32)]
    else:
        out_specs = [row_spec(lambda i: (i, 0))]
        out_shape = [jax.ShapeDtypeStruct((N_ROWS, D_MODEL), F32)]
    n_act = len(out_specs)
    if convert:
        out_specs += [
            pl.BlockSpec((D_MODEL, FFN_FC), lambda i: (0, chunk(i))),
            pl.BlockSpec((D_MODEL, FFN_FC), lambda i: (0, chunk(i))),
            pl.BlockSpec((FFN_FC, D_MODEL), lambda i: (chunk(i), 0)),
        ]
        out_shape += [jax.ShapeDtypeStruct((D_MODEL, D_FF), BF16),
                      jax.ShapeDtypeStruct((D_MODEL, D_FF), BF16),
                      jax.ShapeDtypeStruct((D_FF, D_MODEL), BF16)]
    n_cvt = len(out_specs)
    if mix:
        out_specs += [
            pl.BlockSpec((MIX_IN_ROWS, D_MODEL), lambda i: (in_chunk(i), 0)),
            pl.BlockSpec((MIX_OUT_ROWS, D_MODEL), lambda i: (out_chunk(i), 0)),
        ]
        out_shape += [jax.ShapeDtypeStruct((D_IN, D_MODEL), BF16),
                      jax.ShapeDtypeStruct((D_MODEL, D_MODEL), BF16)]

    res = pl.pallas_call(
        functools.partial(_ffn_kernel, first=first, last=last, convert=convert, mix=mix,
                          np_tiles=np_tiles),
        grid=(N_ROWS // tm,),
        in_specs=in_specs,
        out_specs=out_specs,
        out_shape=out_shape,
        scratch_shapes=[pltpu.VMEM((tm, D_MODEL), F32)] if first else [],
        input_output_aliases={} if (first or last) else {0: 0},
        compiler_params=pltpu.CompilerParams(
            dimension_semantics=("arbitrary",), vmem_limit_bytes=VMEM_LIMIT),
        name=f"ffn_{sub}",
    )(*args)
    return res[:n_act], tuple(res[n_act:n_cvt]), tuple(res[n_cvt:])


def _grouped(x):
    return x.reshape(x.shape[0] // SUBLANES, SUBLANES, x.shape[1])


def _sub_iota(x3):
    return lax.broadcasted_iota(jnp.int32, x3.shape, 1)


def _groups_down(x3, q, first=None):
    if q == 0:
        return x3
    head = jnp.zeros((q,) + x3.shape[1:], x3.dtype) if first is None else first
    return jnp.concatenate([head, x3[:x3.shape[0] - q]], axis=0)


def _groups_up(x3, q):
    if q == 0:
        return x3
    return jnp.concatenate([x3[q:], jnp.zeros((q,) + x3.shape[1:], x3.dtype)], axis=0)


def _shift_down(x, k, first=None, in_group=False):
    x3 = _grouped(x)
    q, r = divmod(k, SUBLANES)
    if r == 0:
        y = _groups_down(x3, q)
    else:
        xr = pltpu.roll(x3, r, 1)
        if in_group:
            y = xr
        else:
            fr = None if first is None else pltpu.roll(_grouped(first), r, 1)
            y = jnp.where(_sub_iota(x3) < r, _groups_down(xr, q + 1, fr if q == 0 else None),
                          _groups_down(xr, q))
    return y.reshape(x.shape)


def _shift_up(x, k, in_group=False):
    x3 = _grouped(x)
    q, r = divmod(k, SUBLANES)
    if r == 0:
        y = _groups_up(x3, q)
    else:
        xr = pltpu.roll(x3, SUBLANES - r, 1)
        if in_group:
            y = xr
        else:
            y = jnp.where(_sub_iota(x3) >= SUBLANES - r, _groups_up(xr, q + 1), _groups_up(xr, q))
    return y.reshape(x.shape)


def _row_in_seg(shape, seg):
    return lax.broadcasted_iota(jnp.int32, shape, 0) & (seg - 1)


def _scan_affine8(a, b):
    t = _row_in_seg(a.shape, SUBLANES)
    d = 1
    while d < SUBLANES:
        m = t >= d
        b = jnp.where(m, a * _shift_down(b, d, in_group=True) + b, b)
        a = jnp.where(m, a * _shift_down(a, d, in_group=True), a)
        d *= 2
    return a, b


def _seg_cumsum(x, seg):
    t = _row_in_seg(x.shape, seg)
    d = 1
    while d < seg:
        x = jnp.where(t >= d, x + _shift_down(x, d, in_group=(seg == SUBLANES)), x)
        d *= 2
    return x


def _causal_conv(xl, convw_ref, bias, shifted):
    xc = None
    for jj in range(CONV_W):
        kk = CONV_W - 1 - jj
        term = (xl if kk == 0 else shifted(kk)) * convw_ref[jj:jj + 1, :]
        xc = term if xc is None else xc + term
    return xc + bias


def _lru_gates(xc, vec_ref, wa_ref, wx_ref):
    half = D_LRU // 2
    xb = xc.astype(BF16)
    xb0, xb1 = xb[:, :half], xb[:, half:]
    ba, bx, lam = vec_ref[1:2, :], vec_ref[2:3, :], vec_ref[3:4, :]
    r = jax.nn.sigmoid(
        jnp.concatenate([_dot(xb0, wa_ref[0]), _dot(xb1, wa_ref[1])], axis=1) + ba)
    i = jax.nn.sigmoid(
        jnp.concatenate([_dot(xb0, wx_ref[0]), _dot(xb1, wx_ref[1])], axis=1) + bx)
    log_a = -LRU_C * r * _softplus(-lam)
    a = jnp.exp(log_a)
    m2 = -jnp.tanh(log_a) * (a * a + 1.0)
    mult = jnp.where(m2 > 0.0, m2 * lax.rsqrt(m2), 0.0)
    return a, mult, i


def _lru_carry(a8, h8, hin):
    hs = []
    for gidx in range(a8.shape[0] // SUBLANES):
        rows = slice(SUBLANES * gidx, SUBLANES * (gidx + 1))
        hg = a8[rows] * hin + h8[rows]
        hin = hg[SUBLANES - 1:SUBLANES, :]
        hs.append(hg)
    return hs


def _level_masks(nlev):
    t = lax.broadcasted_iota(jnp.int32, (1, BLK, BLK), 1)
    s = lax.broadcasted_iota(jnp.int32, (1, BLK, BLK), 2)
    masks = [t == s]
    for lv in range(nlev):
        masks.append((((t >> lv) ^ (s >> lv)) == 1) & (((t >> lv) & 1) == 1))
    return masks


def _head_halves(val, dtype):
    second = (lax.broadcasted_iota(jnp.int32, val.shape, 1) & (LANES - 1)) >= GLA_DK
    zero = jnp.zeros_like(val)
    return (jnp.where(second, zero, val).astype(dtype), jnp.where(second, val, zero).astype(dtype))


def _gla_operands(q, k, gk, seg, state_dtype):
    t = lax.broadcasted_iota(jnp.int32, gk.shape, 0)
    b = _seg_cumsum(gk, seg)
    qs = [_head_halves(q, BF16)]
    ks = [k.astype(BF16)]
    ref = b - gk
    end = b
    s = 1
    while s < seg:
        qs.append(_head_halves(q * jnp.exp(b - ref), BF16))
        ks.append((k * jnp.exp(end - b)).astype(BF16))
        bit = (t & s) != 0
        ref = jnp.where(bit, _shift_down(ref, s, in_group=True), ref)
        end = jnp.where(bit, end, _shift_up(end, s, in_group=True))
        s *= 2
    qe = _head_halves(q * jnp.exp(b), state_dtype)
    ke = _head_halves(k * jnp.exp(end - b), state_dtype)
    return qs, ks, qe, ke, end


def _gla_operands_bounded(q, k, gk):
    n, c = gk.shape
    b = _seg_cumsum(gk, GLA_CHUNK)
    b3 = b.reshape(n // GLA_CHUNK, GLA_CHUNK, c)
    total = jnp.broadcast_to(b3[:, GLA_CHUNK - 1:GLA_CHUNK, :], b3.shape).reshape(n, c)
    qe = _head_halves(q * jnp.exp(b), BF16)
    ke = _head_halves(k * jnp.exp(total - b), BF16)
    kf = (k * jnp.exp(-b)).astype(BF16)
    return qe, ke, kf, total


def _chunk_decay_floor(gk):
    n, c = gk.shape
    return jnp.min(jnp.sum(gk.reshape(n // GLA_CHUNK, GLA_CHUNK, c), axis=1))


def _blocks(x, lanes):
    return x[:, lanes].reshape(x.shape[0] // BLK, BLK, LANES)


def _scores_bounded(qe, kf, h):
    grp, half = divmod(h, 2)
    lanes = slice(LANES * grp, LANES * (grp + 1))
    t = lax.broadcasted_iota(jnp.int32, (1, BLK, BLK), 1)
    s = lax.broadcasted_iota(jnp.int32, (1, BLK, BLK), 2)
    sc = _bdot('ctd,csd->cts', _blocks(qe[half], lanes), _blocks(kf, lanes))
    return jnp.where(t >= s, sc, 0.0).astype(BF16)


def _scores(qs, ks, masks, h):
    grp, half = divmod(h, 2)
    lanes = slice(LANES * grp, LANES * (grp + 1))
    p = None
    for lv, m in enumerate(masks):
        sc = _bdot('ctd,csd->cts', _blocks(qs[lv][half], lanes), _blocks(ks[lv], lanes))
        p = jnp.where(m, sc, 0.0 if p is None else p)
    return p.astype(BF16)


def _decay_rows(bl_row):
    return jnp.exp(jnp.transpose(jnp.broadcast_to(bl_row, (LANES, LANES))))


def _gla_finish(o, g, gn):
    outs = []
    for h in range(GLA_HEADS):
        sl = slice(GLA_DV * h, GLA_DV * (h + 1))
        outs.append(_rms(o[:, sl], gn) * jax.nn.silu(g[:, sl]))
    return jnp.concatenate(outs, axis=1)


def _gk_from(gd, wgate_ref, bgate_ref):
    z = _dot(gd.astype(BF16), wgate_ref[...]) + bgate_ref[...]
    return -_softplus(-z) * (1.0 / GLA_NORMALIZER)


def _out_proj(x, lru_out, gla_out, w_out_ref, gpost_ref):
    m = (_dot(lru_out.astype(BF16), w_out_ref[:D_LRU, :])
         + _dot(gla_out.astype(BF16), w_out_ref[D_LRU:, :]))
    return x + _rms(m, gpost_ref[...])


def _prompt_mixer_kernel(
        x_ref, gpre_ref, gpost_ref, w_lru_ref, w_qk_ref, w_v_ref, w_g_ref, w_gd_ref,
        convw_ref, vec_ref, wa_ref, wx_ref, wgate_ref, bgate_ref, gn_ref, w_out_ref,
        o_ref, hlast_ref, ctail_ref, sfin_ref,
        tail_sc, h_sc, s_sc, x_sc, xy_sc, qk_sc, gk_sc, v_sc, g_sc, bounded_sc):
    s = pl.program_id(0)
    tt = PROMPT_TT
    nt = SEQ // tt
    nchunk = tt // BLK
    seq_start = lax.rem(s + nt - 1, nt) == 0

    @pl.when(s == 0)
    def _():
        bounded_sc[0] = 1

    @pl.when(seq_start)
    def _():
        tail_sc[...] = jnp.zeros_like(tail_sc)
        h_sc[...] = jnp.zeros_like(h_sc)
        s_sc[...] = jnp.zeros_like(s_sc)

    def body(bounded, project=True, finish=True):
        if project:
            x_new = x_ref[...]
            hn = _rms(x_new, gpre_ref[...]).astype(BF16)

        if finish:
            xy = xy_sc[...]
            xl, yl = xy[:, :D_LRU], xy[:, D_LRU:]
            tail = tail_sc[...]
            xc = _causal_conv(xl, convw_ref, vec_ref[0:1, :],
                              lambda kk: _shift_down(xl, kk, first=tail))
            tail_sc[...] = xl[tt - SUBLANES:, :]
            ctail_ref[...] = xl[tt - SUBLANES:, :]
        if project:
            xy_sc[...] = _dot_nt(hn, w_lru_ref[...])
        if finish:
            a, mult, gi = _lru_gates(xc, vec_ref, wa_ref, wx_ref)

            qk = qk_sc[...]
            q, k = qk[:, :D_QK] * (GLA_DK ** -0.5), qk[:, D_QK:]
            gk = gk_sc[...]
            if bounded:
                qe, ke, kf, bl = _gla_operands_bounded(q, k, gk)
            else:
                qs, ks, qe, ke, bl = _gla_operands(q, k, gk, GLA_CHUNK, BF16)
                masks = _level_masks(6)
        if project:
            qk_sc[...] = _dot_nt(hn, w_qk_ref[...])
            gk_new = _gk_from(_dot_nt(hn, w_gd_ref[...]), wgate_ref, bgate_ref)
            gk_sc[...] = gk_new
            bounded_sc[0] = (_chunk_decay_floor(gk_new) >= -GLA_FAST_LIMIT).astype(jnp.int32)

        if finish:
            first_row = (lax.broadcasted_iota(jnp.int32, a.shape, 0) == 0) & seq_start
            mult_r = jnp.where(first_row, 1.0, mult)
            a8, h8 = _scan_affine8(a, mult_r * gi * xc)
            hs = _lru_carry(a8, h8, h_sc[SUBLANES - 1:SUBLANES, :])
            h_sc[...] = hs[-1]
            hlast_ref[...] = hs[-1]
            lru_out = _rms(jnp.concatenate(hs, axis=0) * jax.nn.gelu(yl), vec_ref[4:5, :])

            vb = v_sc[...]
            ps, vhs, upds = [], [], []
            for h in range(GLA_HEADS):
                grp, half = divmod(h, 2)
                lanes = slice(LANES * grp, LANES * (grp + 1))
                ps.append(_scores_bounded(qe, kf, h) if bounded else _scores(qs, ks, masks, h))
                vhs.append(vb[:, GLA_DV * h:GLA_DV * (h + 1)].reshape(nchunk, BLK, GLA_DV))
                upds.append(_bdot('csd,csv->cdv', _blocks(ke[half], lanes), vhs[h]))
        if project:
            v_sc[...] = _dot_nt(hn, w_v_ref[...]).astype(BF16)
            g_new = _dot_nt(hn, w_g_ref[...])

        if finish:
            decs = [[_decay_rows(bl[c * BLK:c * BLK + 1, LANES * grp:LANES * (grp + 1)])
                     for c in range(nchunk)] for grp in range(GLA_HEADS // 2)]
            outs = []
            for h in range(GLA_HEADS):
                grp, half = divmod(h, 2)
                lanes = slice(LANES * grp, LANES * (grp + 1))
                st = s_sc[h]
                states = []
                for c in range(nchunk):
                    states.append(st.astype(BF16))
                    st = decs[grp][c] * st + upds[h][c]
                s_sc[h] = st
                sfin_ref[h] = st[GLA_DK * half:GLA_DK * (half + 1), :]
                o = (_bdot('cts,csv->ctv', ps[h], vhs[h])
                     + _bdot('ctd,cdv->ctv', _blocks(qe[half], lanes), jnp.stack(states)))
                outs.append(o.reshape(tt, GLA_DV))
            gla_out = _gla_finish(jnp.concatenate(outs, axis=1), g_sc[...], gn_ref[...])
        if project:
            g_sc[...] = g_new
        if finish:
            o_ref[...] = _out_proj(x_sc[...], lru_out, gla_out, w_out_ref, gpost_ref)
        if project:
            x_sc[...] = x_new

    last = s == BATCH * nt
    robust = (bounded_sc[0] == 0).astype(jnp.int32)
    branch = jnp.where(s == 0, 0, jnp.where(last, 3, 1) + robust)
    lax.switch(branch, [
        functools.partial(body, True, finish=False),
        functools.partial(body, True), functools.partial(body, False),
        functools.partial(body, True, project=False), functools.partial(body, False, project=False),
    ])


def _weight_specs(l, const):
    d = D_MODEL
    return [
        pl.BlockSpec((None, None, 1, d), const(l, 1, 0, 0)),
        pl.BlockSpec((None, None, 1, d), const(l, 1, 0, 0)),
        pl.BlockSpec((2 * D_LRU, d), const(0, 0)),
        pl.BlockSpec((2 * D_QK, d), const(2, 0)),
        pl.BlockSpec((D_GLA, d), const(3, 0)),
        pl.BlockSpec((D_GLA, d), const(4, 0)),
        pl.BlockSpec((GLA_RANK, d), const((D_IN - GLA_RANK) // GLA_RANK, 0)),
        pl.BlockSpec((None, CONV_W, D_LRU), const(l, 0, 0)),
        pl.BlockSpec((None, 5, D_LRU), const(l, 0, 0)),
        pl.BlockSpec((None, 2, MXU_DIM, MXU_DIM), const(l, 0, 0, 0)),
        pl.BlockSpec((None, 2, MXU_DIM, MXU_DIM), const(l, 0, 0, 0)),
        pl.BlockSpec((None, GLA_RANK, D_QK), const(l, 0, 0)),
        pl.BlockSpec((None, 1, D_QK), const(l, 0, 0)),
        pl.BlockSpec((None, 1, GLA_DV), const(l, 0, 0)),
        pl.BlockSpec((d, d), const(0, 0)),
    ]


def _prompt_mixer_call(x_all, weights, l):
    tt = PROMPT_TT
    nt = SEQ // tt
    ntiles = BATCH * nt
    const = lambda *ix: (lambda s: ix)
    done = lambda s: jnp.maximum(s - 1, 0)
    out_shapes = (
        jax.ShapeDtypeStruct((N_ROWS, D_MODEL), F32),
        jax.ShapeDtypeStruct((BATCH, SUBLANES, D_LRU), F32),
        jax.ShapeDtypeStruct((BATCH, SUBLANES, D_LRU), F32),
        jax.ShapeDtypeStruct((BATCH, GLA_HEADS, GLA_DK, GLA_DV), F32),
    )
    out_specs = (
        pl.BlockSpec((tt, D_MODEL), lambda s: (done(s), 0)),
        pl.BlockSpec((None, SUBLANES, D_LRU), lambda s: (done(s) // nt, 0, 0)),
        pl.BlockSpec((None, SUBLANES, D_LRU), lambda s: (done(s) // nt, 0, 0)),
        pl.BlockSpec((None, GLA_HEADS, GLA_DK, GLA_DV), lambda s: (done(s) // nt, 0, 0, 0)),
    )
    scratch = [
        pltpu.VMEM((SUBLANES, D_LRU), F32),
        pltpu.VMEM((SUBLANES, D_LRU), F32),
        pltpu.VMEM((GLA_HEADS, LANES, GLA_DV), F32),
        pltpu.VMEM((tt, D_MODEL), F32),
        pltpu.VMEM((tt, 2 * D_LRU), F32),
        pltpu.VMEM((tt, 2 * D_QK), F32),
        pltpu.VMEM((tt, D_QK), F32),
        pltpu.VMEM((tt, D_GLA), BF16),
        pltpu.VMEM((tt, D_GLA), F32),
        pltpu.SMEM((1,), jnp.int32),
    ]
    return pl.pallas_call(
        _prompt_mixer_kernel,
        grid=(ntiles + 1,),
        in_specs=[pl.BlockSpec((tt, D_MODEL), lambda s: (jnp.minimum(s, ntiles - 1), 0))]
                 + _weight_specs(l, const),
        out_specs=out_specs,
        out_shape=out_shapes,
        scratch_shapes=scratch,
        input_output_aliases={0: 0},
        compiler_params=pltpu.CompilerParams(
            dimension_semantics=("arbitrary",), vmem_limit_bytes=VMEM_LIMIT),
        name="mixer_prompt",
    )(x_all, *weights)


def _sample_mixer_kernel(
        x_ref, gpre_ref, gpost_ref, w_lru_ref, w_qk_ref, w_v_ref, w_g_ref, w_gd_ref,
        convw_ref, vec_ref, wa_ref, wx_ref, wgate_ref, bgate_ref, gn_ref, w_out_ref,
        h0_ref, tail_ref, s0_ref, *rest, carried):
    o_ref, hseq_ref, xl_ref, snew_ref = rest[1:] if carried else rest
    rows = SAMPLE_BT * DEC_SEQ
    nseq = SAMPLE_BT
    x = x_ref[...]
    hn = _rms(x, gpre_ref[...]).astype(BF16)
    xy = _dot_nt(hn, w_lru_ref[...])
    xl, yl = xy[:, :D_LRU], xy[:, D_LRU:]
    qk = _dot_nt(hn, w_qk_ref[...])
    q, k = qk[:, :D_QK] * (GLA_DK ** -0.5), qk[:, D_QK:]
    v = _dot_nt(hn, w_v_ref[...])
    g = _dot_nt(hn, w_g_ref[...])
    gd = _dot_nt(hn, w_gd_ref[...])

    t8 = _row_in_seg(xl.shape, DEC_SEQ)
    tail = tail_ref[...]
    xc = _causal_conv(
        xl, convw_ref, vec_ref[0:1, :],
        lambda kk: jnp.where(t8 < kk, _shift_down(tail, kk, in_group=True),
                             _shift_down(xl, kk, in_group=True)))
    xl_ref[...] = xl

    a, mult, gi = _lru_gates(xc, vec_ref, wa_ref, wx_ref)
    bterm = mult * gi * xc + a * h0_ref[...]
    _, hseq = _scan_affine8(a, bterm)
    hseq_ref[...] = hseq
    lru_out = _rms(hseq * jax.nn.gelu(yl), vec_ref[4:5, :])

    gk = _gk_from(gd, wgate_ref, bgate_ref)
    qs, ks, qe, ke, bl = _gla_operands(q, k, gk, DEC_SEQ, F32)
    masks = _level_masks(3)
    vb = v.astype(BF16)
    zeros_half = jnp.zeros((nseq, GLA_DK, GLA_DV), F32)

    def per_seq(x2):
        return x2.reshape(nseq, DEC_SEQ, x2.shape[1]).astype(BF16)

    decs = []
    for grp in range(GLA_HEADS // 2):
        bl3 = bl[:, LANES * grp:LANES * (grp + 1)].reshape(nseq, DEC_SEQ, LANES)
        decs.append([_decay_rows(bl3[b, 0:1, :]) for b in range(nseq)])
    outs = []
    for h in range(GLA_HEADS):
        grp, half = divmod(h, 2)
        lanes = slice(LANES * grp, LANES * (grp + 1))
        cols = slice(GLA_DV * h, GLA_DV * (h + 1))
        ks_rows = slice(GLA_DK * half, GLA_DK * (half + 1))
        p = _scores(qs, ks, masks, h)
        vh = vb[:, cols].reshape(rows // BLK, BLK, GLA_DV)
        o_intra = _bdot('cts,csv->ctv', p, vh).reshape(rows, GLA_DV)
        s0 = s0_ref[:, h]
        s_pad = jnp.concatenate([zeros_half, s0] if half else [s0, zeros_half], axis=1)
        o_inter = _bdot('btd,bdv->btv', per_seq(qe[half][:, lanes]), s_pad.astype(BF16))
        outs.append(o_intra + o_inter.reshape(rows, GLA_DV))
        upd = _bdot('bsd,bsv->bdv', per_seq(ke[half][:, lanes]), per_seq(v[:, cols]))
        for b in range(nseq):
            snew_ref[b, h] = decs[grp][b][ks_rows] * s0[b] + upd[b, ks_rows, :]
    gla_out = _gla_finish(jnp.concatenate(outs, axis=1), g, gn_ref[...])

    o_ref[...] = _out_proj(x, lru_out, gla_out, w_out_ref, gpost_ref)


def _sample_mixer_call(x_all, weights, h0pad, tailpad, s0, snew_all, l):
    rows = SAMPLE_BT * DEC_SEQ
    nb = DEC_BATCH // SAMPLE_BT
    base = N_PROMPT // rows
    const = lambda *ix: (lambda i: ix)
    st_spec = pl.BlockSpec((None, SAMPLE_BT, GLA_HEADS, GLA_DK, GLA_DV),
                           lambda i: (l, i, 0, 0, 0))
    out_shapes = (
        jax.ShapeDtypeStruct((N_ROWS, D_MODEL), F32),
        jax.ShapeDtypeStruct((N_SAMPLE, D_LRU), F32),
        jax.ShapeDtypeStruct((N_SAMPLE, D_LRU), F32),
        jax.ShapeDtypeStruct((DEPTH, DEC_BATCH, GLA_HEADS, GLA_DK, GLA_DV), F32),
    )
    out_specs = (
        pl.BlockSpec((rows, D_MODEL), lambda i: (base + i, 0)),
        pl.BlockSpec((rows, D_LRU), lambda i: (i, 0)),
        pl.BlockSpec((rows, D_LRU), lambda i: (i, 0)),
        st_spec,
    )
    in_specs = ([pl.BlockSpec((rows, D_MODEL), lambda i: (base + i, 0))]
                + _weight_specs(l, const)
                + [pl.BlockSpec((None, rows, D_LRU), lambda i: (l, i, 0)),
                   pl.BlockSpec((None, rows, D_LRU), lambda i: (l, i, 0)),
                   st_spec])
    args = [x_all, *weights, h0pad, tailpad, s0]
    aliases = {0: 0}
    if snew_all is not None:
        aliases[len(args)] = 3
        in_specs.append(pl.BlockSpec(memory_space=pl.ANY))
        args.append(snew_all)
    return pl.pallas_call(
        functools.partial(_sample_mixer_kernel, carried=snew_all is not None),
        grid=(nb,),
        in_specs=in_specs,
        out_specs=out_specs,
        out_shape=out_shapes,
        input_output_aliases=aliases,
        compiler_params=pltpu.CompilerParams(
            dimension_semantics=("parallel",), vmem_limit_bytes=VMEM_LIMIT),
        name="mixer_sample",
    )(*args)


def _blockdiag(w):
    per = MXU_DIM // LRU_BLOCK
    w = w.reshape(DEPTH, N_LRU_BLOCKS // per, per, LRU_BLOCK, LRU_BLOCK)
    eye = jnp.eye(per, dtype=w.dtype)
    bd = jnp.einsum('lpiab,ij->lpiajb', w, eye)
    return bd.reshape(DEPTH, N_LRU_BLOCKS // per, MXU_DIM, MXU_DIM)


def kernel(x_prompt, x_sample, state_lru_h, state_lru_conv, state_gla, norm_pre, norm_post, w_ffn_gate, w_ffn_up, w_ffn_down, w_in, conv_w, conv_b, lru_wa, lru_ba, lru_wx, lru_bx, lru_lambda, lru_norm, gla_w_gate, gla_b_gate, gla_norm, w_out):
    gpre4 = norm_pre.reshape(DEPTH, 3, 1, D_MODEL)
    gpost4 = norm_post.reshape(DEPTH, 3, 1, D_MODEL)
    w_ffn_f32 = (w_ffn_gate, w_ffn_up, w_ffn_down)
    vec = jnp.stack([conv_b, lru_ba, lru_bx, lru_lambda, lru_norm], axis=1)
    w_in_t = jnp.swapaxes(w_in, 1, 2)
    small = (conv_w, vec,
             _blockdiag(lru_wa).astype(BF16), _blockdiag(lru_wx).astype(BF16),
             gla_w_gate.astype(BF16), gla_b_gate.reshape(DEPTH, 1, D_QK),
             gla_norm.reshape(DEPTH, 1, GLA_DV))

    def mixer_weights(w_in_l, w_out_l):
        return (gpre4, gpost4) + (w_in_l,) * 5 + small + (w_out_l,)
    h0pad = jnp.pad(state_lru_h[:, :, None, :],
                    ((0, 0), (0, 0), (0, DEC_SEQ - 1), (0, 0))).reshape(DEPTH, N_SAMPLE, D_LRU)
    tailpad = jnp.pad(state_lru_conv,
                      ((0, 0), (0, 0), (SUBLANES - (CONV_W - 1), 0), (0, 0))
                      ).reshape(DEPTH, N_SAMPLE, D_LRU)

    w_bf16 = tuple(w[0, 0].astype(BF16) for w in w_ffn_f32)
    ffn_order = [(l, idx) for l in range(DEPTH) for idx in range(2)]

    def ffn(xs, l, idx, first=False, last=False):
        pos = ffn_order.index((l, idx))
        nxt = ffn_order[pos + 1] if pos + 1 < len(ffn_order) else None
        return _ffn_call(xs, gpre4, gpost4, w_bf16, w_ffn_f32 if nxt else None,
                         (w_in_t, w_out) if idx == 0 else None, l, idx, nxt, "ab"[idx], first,
                         last)

    hp, cp, sp, hs, cs = [], [], [], [], []
    snew_all = None
    xs = (x_prompt.reshape(N_PROMPT, D_MODEL), x_sample.reshape(N_SAMPLE, D_MODEL))
    for l in range(DEPTH):
        (x_all,), w_bf16, mix_bf16 = ffn(xs, l, 0, first=(l == 0))
        weights = mixer_weights(*mix_bf16)
        x_all, hl, ct, sf = _prompt_mixer_call(x_all, weights, l)
        hp.append(hl[:, SUBLANES - 1])
        cp.append(ct[:, SUBLANES - (CONV_W - 1):])
        sp.append(sf)
        x_all, hseq, xl, snew_all = _sample_mixer_call(
            x_all, weights, h0pad, tailpad, state_gla, snew_all, l)
        hs.append(hseq.reshape(DEC_BATCH, DEC_SEQ, D_LRU)[:, DEC_SEQ - 1])
        cs.append(xl.reshape(DEC_BATCH, DEC_SEQ, D_LRU)[:, DEC_SEQ - (CONV_W - 1):])
        xs, w_bf16, _ = ffn((x_all,), l, 1, last=(l == DEPTH - 1))

    yp = xs[0].reshape(BATCH, SEQ, D_MODEL)
    ys = xs[1].reshape(DEC_BATCH, DEC_SEQ, D_MODEL)
    return (yp, ys, jnp.stack(hp), jnp.stack(cp), jnp.stack(sp),
            jnp.stack(hs), jnp.stack(cs), snew_all)
```
